```python
import math
import jax
import jax.numpy as jnp
from jax import lax
import numpy as np

D_MODEL = 1024
BATCH = 4
SEQ = 4096
DEPTH = 4
DEC_BATCH = 128
DEC_SEQ = 8
PAST_LEN = 2048
PAGE_SIZE = 128

N_EVEN = (DEPTH + 1) // 2
N_ODD = DEPTH // 2
POOL_WIDTH = D_MODEL // 2
POOL_WINDOWS = (2, 4, 8, 16)
POOL_GROUPS = len(POOL_WINDOWS)
POOL_GROUP_DIM = POOL_WIDTH // POOL_GROUPS
POOL_BUF = max(POOL_WINDOWS) - 1
DA_HEADS = 4
DA_QK_DIM = 64
DA_V_DIM = 2 * DA_QK_DIM
DA_WIDTH = DA_HEADS * DA_V_DIM
DA_QK_WIDTH = DA_HEADS * 2 * DA_QK_DIM
ROT_DIM = DA_QK_DIM // 4
ROPE_THETA = 500000.0
Q_BLOCK = 128
GLA_HEADS = 4
GLA_K_DIM = D_MODEL // 2 // GLA_HEADS
GLA_V_DIM = D_MODEL // GLA_HEADS
GLA_GATE_RANK = 16
GLA_TAU = 16.0
GLA_CHUNK = 64
EVEN_IN = 2 * POOL_WIDTH + 2 * DA_QK_WIDTH + 2 * DA_WIDTH
EVEN_SPLITS = (POOL_WIDTH, 2 * POOL_WIDTH, 2 * POOL_WIDTH + DA_QK_WIDTH,
               2 * POOL_WIDTH + 2 * DA_QK_WIDTH, 2 * POOL_WIDTH + 2 * DA_QK_WIDTH + DA_WIDTH)
EVEN_OUT_IN = POOL_WIDTH + DA_WIDTH
ODD_IN = 2 * GLA_HEADS * GLA_K_DIM + 2 * GLA_HEADS * GLA_V_DIM
ODD_SPLITS = (GLA_HEADS * GLA_K_DIM, 2 * GLA_HEADS * GLA_K_DIM, 2 * GLA_HEADS * GLA_K_DIM + GLA_HEADS * GLA_V_DIM)
ODD_OUT_IN = GLA_HEADS * GLA_V_DIM
DN_ALPHA = (2 * DEPTH) ** 0.25
DN_BETA = (8 * DEPTH) ** -0.25
LN_EPS = 1e-5
RMS_EPS = 1e-5

kernel_name = 'hybrid_pool_diffattn_gla_decoder'


def layer_norm(x, g, b):
    xf = x.astype(jnp.float32)
    mu = jnp.mean(xf, -1, keepdims=True)
    xc = xf - mu
    var = jnp.mean(xc * xc, -1, keepdims=True)
    return (xc * lax.rsqrt(var + LN_EPS) * g.astype(jnp.float32) + b.astype(jnp.float32)).astype(x.dtype)


def rms_norm(x, w):
    xf = x.astype(jnp.float32)
    return xf * lax.rsqrt(jnp.mean(xf * xf, -1, keepdims=True) + RMS_EPS) * w.astype(jnp.float32)


def rope_partial(x, pos):
    half = ROT_DIM // 2
    inv_freq = jnp.power(jnp.float32(ROPE_THETA), -jnp.arange(0, ROT_DIM, 2, dtype=jnp.float32) / ROT_DIM)
    ang = pos.astype(jnp.float32)[:, None] * inv_freq[None, :]
    cos = jnp.cos(ang)[None, :, None, None, :]
    sin = jnp.sin(ang)[None, :, None, None, :]
    xf = x.astype(jnp.float32)
    x1 = xf[..., :half]
    x2 = xf[..., half:ROT_DIM]
    return jnp.concatenate([x1 * cos - x2 * sin, x2 * cos + x1 * sin, xf[..., ROT_DIM:]], -1).astype(x.dtype)


def pool_mixer(u, buf, pos0, w_lin, scale):
    b, t, _ = u.shape
    u_ext = u if buf is None else jnp.concatenate([buf.astype(u.dtype), u], axis=1)
    n_ext = u_ext.shape[1]
    n_past = n_ext - t
    cs = jnp.cumsum(u_ext.astype(jnp.float32), axis=1)
    cs = jnp.pad(cs, ((0, 0), (1, 0), (0, 0))).reshape(b, n_ext + 1, POOL_GROUPS, POOL_GROUP_DIM)
    hi = jnp.arange(t) + n_past + 1
    pos = pos0 + jnp.arange(t)
    uf = u.astype(jnp.float32).reshape(b, t, POOL_GROUPS, POOL_GROUP_DIM)
    outs = []
    for g, w in enumerate(POOL_WINDOWS):
        lo = jnp.maximum(hi - w, 0)
        win_sum = cs[:, hi, g] - cs[:, lo, g]
        count = jnp.minimum(pos + 1, w).astype(jnp.float32)[None, :, None]
        outs.append(win_sum / count - uf[:, :, g])
    d = jnp.stack(outs, axis=2)
    y = jnp.einsum('btgc,gce->btge', d, w_lin.astype(jnp.float32)).reshape(b, t, POOL_WIDTH)
    y = y * scale.astype(jnp.float32)
    return y.astype(u.dtype), u_ext[:, -POOL_BUF:]


def diff_lambda(lam, layer_idx):
    lam = lam.astype(jnp.float32)
    lam_init = 0.8 - 0.6 * math.exp(-0.3 * layer_idx)
    lam_full = jnp.exp(jnp.sum(lam[0] * lam[1])) - jnp.exp(jnp.sum(lam[2] * lam[3])) + lam_init
    return lam_full, lam_init


def diff_attn_prompt(q, k, v, lam_full):
    b, s, h, _, d = q.shape
    qb_len = math.gcd(s, Q_BLOCK)
    n_blk = s // qb_len
    qb = q.reshape(b, n_blk, qb_len, h, 2, d).swapaxes(0, 1)
    kpos = jnp.arange(s)

    def block(args):
        qi, bi = args
        sc = jnp.einsum('bqhcd,bkhcd->bhcqk', qi, k)
        qpos = bi * qb_len + jnp.arange(qb_len)
        sc = jnp.where((qpos[:, None] >= kpos[None, :])[None, None, None], sc, -jnp.inf)
        p = jax.nn.softmax(sc, axis=-1)
        a = p[:, :, 0] - lam_full * p[:, :, 1]
        return jnp.einsum('bhqk,bkhe->bqhe', a, v)

    o = lax.map(block, (qb, jnp.arange(n_blk)))
    return o.swapaxes(0, 1).reshape(b, s, h, v.shape[-1])


def diff_attn_sample(q, k_new, v_new, k_past, v_past, lam_full):
    t = q.shape[1]
    n_past = k_past.shape[1]
    s_past = jnp.einsum('nqhcd,nkhcd->nhcqk', q, k_past)
    s_new = jnp.einsum('nqhcd,nkhcd->nhcqk', q, k_new)
    causal = jnp.arange(t)[:, None] >= jnp.arange(t)[None, :]
    s_new = jnp.where(causal[None, None, None], s_new, -jnp.inf)
    p = jax.nn.softmax(jnp.concatenate([s_past, s_new], axis=-1), axis=-1)
    a = p[:, :, 0] - lam_full * p[:, :, 1]
    return (jnp.einsum('nhqk,nkhe->nqhe', a[..., :n_past], v_past)
            + jnp.einsum('nhqk,nkhe->nqhe', a[..., n_past:], v_new))


def even_layer(x, pos0, layer_idx, pool_buf, k_past, v_past,
               w_in, w_pool_lin, pool_scale, lam, subln_w, w_out, ln_g, ln_b):
    b, t, _ = x.shape
    pu, pg, q, k, v, ag = jnp.split(x @ w_in, EVEN_SPLITS, axis=-1)
    pool_out, new_buf = pool_mixer(pu, pool_buf, pos0, w_pool_lin, pool_scale)
    pos = pos0 + jnp.arange(t)
    q = rope_partial(q.reshape(b, t, DA_HEADS, 2, DA_QK_DIM), pos) * DA_QK_DIM ** -0.5
    k = rope_partial(k.reshape(b, t, DA_HEADS, 2, DA_QK_DIM), pos)
    v = v.reshape(b, t, DA_HEADS, DA_V_DIM)
    lam_full, lam_init = diff_lambda(lam, layer_idx)
    qf, kf, vf = q.astype(jnp.float32), k.astype(jnp.float32), v.astype(jnp.float32)
    if k_past is None:
        o = diff_attn_prompt(qf, kf, vf, lam_full)
    else:
        o = diff_attn_sample(qf, kf, vf, k_past.astype(jnp.float32), v_past.astype(jnp.float32), lam_full)
    o = (rms_norm(o, subln_w) * (1.0 - lam_init)).reshape(b, t, DA_WIDTH).astype(x.dtype)
    mix = jnp.concatenate([pool_out * jax.nn.silu(pg), o * jax.nn.silu(ag)], axis=-1)
    y = layer_norm(DN_ALPHA * x + mix @ w_out, ln_g, ln_b)
    return y, new_buf, k.reshape(b, t, DA_HEADS, 2 * DA_QK_DIM), v


def gla_chunked(q, k, v, log_a, s0):
    b, l, h, _ = q.shape
    dv = v.shape[-1]
    c = math.gcd(l, GLA_CHUNK)
    n = l // c

    def to_chunks(z):
        return z.astype(jnp.float32).reshape(b, n, c, h, z.shape[-1]).swapaxes(0, 1)

    causal = jnp.arange(c)[:, None] >= jnp.arange(c)[None, :]

    def step(s, inp):
        qc, kc, vc, gc = inp
        cb = jnp.cumsum(gc, axis=1)
        cb_last = cb[:, -1]
        o_inter = jnp.einsum('bihk,bhkv->bihv', qc * jnp.exp(cb), s)
        rel = jnp.where(causal[None, :, :, None, None], cb[:, :, None] - cb[:, None, :], -jnp.inf)
        att = jnp.sum(qc[:, :, None] * kc[:, None] * jnp.exp(rel), axis=-1)
        o_intra = jnp.einsum('bijh,bjhv->bihv', att, vc)
        s_new = jnp.exp(cb_last)[..., None] * s + jnp.einsum(
            'bjhk,bjhv->bhkv', kc * jnp.exp(cb_last[:, None] - cb), vc)
        return s_new, o_inter + o_intra

    s_end, o = lax.scan(step, s0, (to_chunks(q), to_chunks(k), to_chunks(v), to_chunks(log_a)))
    return o.swapaxes(0, 1).reshape(b, l, h, dv), s_end


def odd_layer(x, s0, w_in, w_ga, w_gb, b_g, norm_w, w_out, ln_g, ln_b):
    b, t, _ = x.shape
    q, k, v, g = jnp.split(x @ w_in, ODD_SPLITS, axis=-1)
    log_a = jax.nn.log_sigmoid(((x @ w_ga) @ w_gb + b_g).astype(jnp.float32)) / GLA_TAU
    q = q.reshape(b, t, GLA_HEADS, GLA_K_DIM) * GLA_K_DIM ** -0.5
    k = k.reshape(b, t, GLA_HEADS, GLA_K_DIM)
    v = v.reshape(b, t, GLA_HEADS, GLA_V_DIM)
    log_a = log_a.reshape(b, t, GLA_HEADS, GLA_K_DIM)
    if s0 is None:
        s0 = jnp.zeros((b, GLA_HEADS, GLA_K_DIM, GLA_V_DIM), jnp.float32)
    o, s_end = gla_chunked(q, k, v, log_a, s0.astype(jnp.float32))
    o = rms_norm(o, norm_w).reshape(b, t, ODD_OUT_IN).astype(x.dtype) * jax.nn.silu(g)
    y = layer_norm(DN_ALPHA * x + o @ w_out, ln_g, ln_b)
    return y, s_end.astype(x.dtype)


def setup_inputs(seed: int = 0) -> dict:
    key = jax.random.key(seed)
    ks = jax.random.split(key, 24)

    def nrm(k, shape, scale=1.0):
        return jax.random.normal(k, shape, jnp.float32) * scale

    n_pages = PAST_LEN // PAGE_SIZE
    n_used = DEC_BATCH * n_pages
    n_pool = n_used + max(1, n_used // 4)
    page_table = jax.random.permutation(ks[0], n_pool)[:n_used].reshape(DEC_BATCH, n_pages).astype(jnp.int32)
    return {
        'x_prompt': nrm(ks[1], (BATCH, SEQ, D_MODEL)),
        'x_sample': nrm(ks[2], (DEC_BATCH, DEC_SEQ, D_MODEL)),
        'cache_k': nrm(ks[3], (n_pool, N_EVEN, PAGE_SIZE, DA_HEADS, 2 * DA_QK_DIM)),
        'cache_v': nrm(ks[4], (n_pool, N_EVEN, PAGE_SIZE, DA_HEADS, DA_V_DIM)),
        'state_pool': nrm(ks[5], (DEC_BATCH, N_EVEN, POOL_BUF, POOL_WIDTH)),
        'state_gla': nrm(ks[6], (DEC_BATCH, N_ODD, GLA_HEADS, GLA_K_DIM, GLA_V_DIM)),
        'page_table': page_table,
        'w_in_even': nrm(ks[7], (N_EVEN, D_MODEL, EVEN_IN), D_MODEL ** -0.5),
        'w_pool_lin': nrm(ks[8], (N_EVEN, POOL_GROUPS, POOL_GROUP_DIM, POOL_GROUP_DIM), POOL_GROUP_DIM ** -0.5),
        'pool_scale': 1.0 + nrm(ks[9], (N_EVEN, POOL_WIDTH), 0.1),
        'diff_lambda_params': nrm(ks[10], (N_EVEN, 4, DA_QK_DIM), 0.1),
        'subln_w': 1.0 + nrm(ks[11], (N_EVEN, DA_V_DIM), 0.1),
        'w_out_even': nrm(ks[12], (N_EVEN, EVEN_OUT_IN, D_MODEL), EVEN_OUT_IN ** -0.5 * DN_BETA),
        'w_in_odd': nrm(ks[13], (N_ODD, D_MODEL, ODD_IN), D_MODEL ** -0.5),
        'w_gate_a': nrm(ks[14], (N_ODD, D_MODEL, GLA_GATE_RANK), D_MODEL ** -0.5),
        'w_gate_b': nrm(ks[15], (N_ODD, GLA_GATE_RANK, GLA_HEADS * GLA_K_DIM), GLA_GATE_RANK ** -0.5),
        'b_gate': nrm(ks[16], (N_ODD, GLA_HEADS * GLA_K_DIM), 0.1),
        'gla_norm_w': 1.0 + nrm(ks[17], (N_ODD, GLA_V_DIM), 0.1),
        'w_out_odd': nrm(ks[18], (N_ODD, ODD_OUT_IN, D_MODEL), ODD_OUT_IN ** -0.5 * DN_BETA),
        'ln_g': 1.0 + nrm(ks[19], (DEPTH, D_MODEL), 0.1),
        'ln_b': nrm(ks[20], (DEPTH, D_MODEL), 0.1),
    }


def reference(x_prompt, x_sample, cache_k, cache_v, state_pool, state_gla, page_table,
              w_in_even, w_pool_lin, pool_scale, diff_lambda_params, subln_w, w_out_even,
              w_in_odd, w_gate_a, w_gate_b, b_gate, gla_norm_w, w_out_odd, ln_g, ln_b):
    n_seq, n_pages = page_table.shape
    past_len = n_pages * cache_k.shape[2]
    yp, ys = x_prompt, x_sample
    kp, vp, pp, gp = [], [], [], []
    ksm, vsm, psm, gsm = [], [], [], []
    for i in range(DEPTH):
        j = i // 2
        if i % 2 == 0:
            ew = (w_in_even[j], w_pool_lin[j], pool_scale[j], diff_lambda_params[j], subln_w[j],
                  w_out_even[j], ln_g[i], ln_b[i])
            yp, buf_p, k_p, v_p = even_layer(yp, 0, i, None, None, None, *ew)
            k_past = cache_k[page_table, j].reshape(n_seq, past_len, DA_HEADS, 2, DA_QK_DIM)
            v_past = cache_v[page_table, j].reshape(n_seq, past_len, DA_HEADS, DA_V_DIM)
            ys, buf_s, k_s, v_s = even_layer(ys, past_len, i, state_pool[:, j], k_past, v_past, *ew)
            kp.append(k_p)
            vp.append(v_p)
            pp.append(buf_p)
            ksm.append(k_s)
            vsm.append(v_s)
            psm.append(buf_s)
        else:
            ow = (w_in_odd[j], w_gate_a[j], w_gate_b[j], b_gate[j], gla_norm_w[j], w_out_odd[j], ln_g[i], ln_b[i])
            yp, s_p = odd_layer(yp, None, *ow)
            ys, s_s = odd_layer(ys, state_gla[:, j], *ow)
            gp.append(s_p)
            gsm.append(s_s)
    new_k_prompt = jnp.stack(kp, axis=1)
    new_v_prompt = jnp.stack(vp, axis=1)
    new_pool_prompt = jnp.stack(pp, axis=1)
    new_gla_prompt = jnp.stack(gp, axis=1)
    new_k_sample = jnp.stack(ksm, axis=1)
    new_v_sample = jnp.stack(vsm, axis=1)
    new_pool_sample = jnp.stack(psm, axis=1)
    new_gla_sample = jnp.stack(gsm, axis=1)
    return (yp, ys, new_k_prompt, new_v_prompt, new_pool_prompt, new_gla_prompt,
            new_k_sample, new_v_sample, new_pool_sample, new_gla_sample)
```

```python
import functools
import math

import numpy as np
import jax
import jax.numpy as jnp
from jax import lax
from jax.experimental import pallas as pl
from jax.experimental.pallas import tpu as pltpu

F32 = jnp.float32
BF16 = jnp.bfloat16

LANES = 128
SUBLANES = 8
VMEM_LIMIT_BYTES = 56 * 1024 * 1024

D_MODEL = 1024
DEPTH = 4
POOL_WIDTH = 512
POOL_WINDOWS = (2, 4, 8, 16)
POOL_GROUP_DIM = 128
POOL_HIST = 16
DA_HEADS = 4
DA_QK_DIM = 64
DA_V_DIM = 128
DA_WIDTH = 512
ROT_DIM = 16
ROPE_THETA = 500000.0
GLA_HEADS = 4
GLA_K_DIM = 128
GLA_V_DIM = 256
GLA_GATE_RANK = 16
GLA_TAU = 16.0
GLA_BLOCK = 64
DN_ALPHA = (2 * DEPTH) ** 0.25
LN_EPS = 1e-5
RMS_EPS = 1e-5

_NT = (((1,), (1,)), ((), ()))
_TN = (((0,), (0,)), ((), ()))


def _dot(a, b):
    return jnp.dot(a, b, preferred_element_type=F32)


def _dot_nt(a, b):
    return lax.dot_general(a, b, _NT, preferred_element_type=F32)


def _dot_tn(a, b):
    return lax.dot_general(a, b, _TN, preferred_element_type=F32)


def _silu(x):
    return x * jax.nn.sigmoid(x)


def _cparams(sem):
    return pltpu.CompilerParams(dimension_semantics=sem, vmem_limit_bytes=VMEM_LIMIT_BYTES)


def _layer_norm_rows(z, g, b):
    mu = jnp.mean(z, axis=-1, keepdims=True)
    zc = z - mu
    var = jnp.mean(zc * zc, axis=-1, keepdims=True)
    return zc * lax.rsqrt(var + LN_EPS) * g + b


def _even_in_kernel(*refs, tn, tr, pos0, has_past):
    if has_past:
        (x_ref, w_ref, wlin_ref, pscale_ref, cos_ref, sina_ref, sinb_ref, buf_ref,
         pmix_ref, q_ref, k32_ref, kbf_ref, v32_ref, vbf_ref, ga_ref, npool_ref, ext_ref) = refs
    else:
        (x_ref, w_ref, wlin_ref, pscale_ref, cos_ref, sina_ref, sinb_ref,
         pmix_ref, q_ref, k32_ref, kbf_ref, v32_ref, vbf_ref, ga_ref, npool_ref, ext_ref) = refs
        buf_ref = None
    r = pl.program_id(1)
    rows = tn * tr
    xb = x_ref[...].astype(BF16)

    def proj(c0, n):
        return _dot(xb, w_ref[:, c0:c0 + n])

    pu = proj(0, POOL_WIDTH)
    pg = proj(POOL_WIDTH, POOL_WIDTH)
    if has_past:
        ext_ref[:, 0:POOL_HIST, :] = buf_ref[...]
    else:
        @pl.when(r == 0)
        def _():
            ext_ref[:, 0:POOL_HIST, :] = jnp.zeros((tn, POOL_HIST, POOL_WIDTH), F32)

        @pl.when(r > 0)
        def _():
            ext_ref[:, 0:POOL_HIST, :] = ext_ref[:, tr:tr + POOL_HIST, :]
    for s in range(tn):
        ext_ref[s, POOL_HIST:POOL_HIST + tr, :] = pu[s * tr:(s + 1) * tr, :]
    npool_ref[...] = ext_ref[:, tr:tr + POOL_HIST, :]

    pos = pos0 + r * tr + lax.broadcasted_iota(jnp.int32, (tr, POOL_GROUP_DIM), 0)
    for g, w in enumerate(POOL_WINDOWS):
        c0 = g * POOL_GROUP_DIM
        cnt = jnp.minimum(pos + 1, w).astype(F32)
        pieces = []
        for s in range(tn):
            u_g = ext_ref[s, POOL_HIST:POOL_HIST + tr, c0:c0 + POOL_GROUP_DIM]
            acc = u_g
            for i in range(1, w):
                acc = acc + ext_ref[s, POOL_HIST - i:POOL_HIST - i + tr, c0:c0 + POOL_GROUP_DIM]
            pieces.append(acc / cnt - u_g)
        d = pieces[0] if tn == 1 else jnp.concatenate(pieces, axis=0)
        y = _dot(d.astype(BF16), wlin_ref[g]) * pscale_ref[:, c0:c0 + POOL_GROUP_DIM]
        pmix_ref[:, c0:c0 + POOL_GROUP_DIM] = (y * _silu(pg[:, c0:c0 + POOL_GROUP_DIM])).astype(pmix_ref.dtype)

    cos = cos_ref[...]
    sina = sina_ref[...]
    sinb = sinb_ref[...]

    def rope(xh):
        return (xh * cos + pltpu.roll(xh, ROT_DIM // 2, 1) * sina
                + pltpu.roll(xh, LANES - ROT_DIM // 2, 1) * sinb)

    q = proj(2 * POOL_WIDTH, DA_WIDTH)
    k = proj(2 * POOL_WIDTH + DA_WIDTH, DA_WIDTH)
    for h in range(DA_HEADS):
        sl = slice(h * LANES, (h + 1) * LANES)
        q_ref[:, sl] = (rope(q[:, sl]) * (DA_QK_DIM ** -0.5)).astype(q_ref.dtype)
        kr = rope(k[:, sl])
        k32_ref[:, sl] = kr
        kbf_ref[:, sl] = kr.astype(kbf_ref.dtype)
    v = proj(2 * POOL_WIDTH + 2 * DA_WIDTH, DA_WIDTH)
    v32_ref[...] = v
    vbf_ref[...] = v.astype(vbf_ref.dtype)
    ga_ref[...] = _silu(proj(2 * POOL_WIDTH + 3 * DA_WIDTH, DA_WIDTH)).astype(ga_ref.dtype)


def _even_in(x2d, n_seq, t_seq, tn, tr, pos0, w_in, wlin, pscale, cos, sina, sinb, buf16, layer_j, act_dtype):
    n_row_tiles = t_seq // tr
    n_seq_tiles = n_seq // tn
    rows = tn * tr
    total = n_seq * t_seq
    has_past = buf16 is not None
    kern = functools.partial(_even_in_kernel, tn=tn, tr=tr, pos0=pos0, has_past=has_past)
    row_map = lambda n, r: (n * n_row_tiles + r, 0)
    const2 = lambda n, r: (0, 0)
    tab_map = (lambda n, r: (r, 0)) if not has_past else const2
    in_specs = [
        pl.BlockSpec((rows, D_MODEL), row_map),
        pl.BlockSpec(w_in.shape, const2),
        pl.BlockSpec(wlin.shape, lambda n, r: (0, 0, 0)),
        pl.BlockSpec(pscale.shape, const2),
        pl.BlockSpec((rows, LANES), tab_map),
        pl.BlockSpec((rows, LANES), tab_map),
        pl.BlockSpec((rows, LANES), tab_map),
    ]
    args = [x2d, w_in, wlin, pscale, cos, sina, sinb]
    if has_past:
        in_specs.append(pl.BlockSpec((tn, None, POOL_HIST, POOL_WIDTH), lambda n, r: (n, layer_j, 0, 0)))
        args.append(buf16)
    blk = lambda: pl.BlockSpec((rows, DA_WIDTH), row_map)
    out_specs = [blk() for _ in range(7)] + [pl.BlockSpec((tn, POOL_HIST, POOL_WIDTH), lambda n, r: (n, 0, 0))]
    sds = lambda dt: jax.ShapeDtypeStruct((total, DA_WIDTH), dt)
    out_shape = [sds(act_dtype), sds(act_dtype), sds(F32), sds(act_dtype), sds(F32), sds(act_dtype), sds(act_dtype),
                 jax.ShapeDtypeStruct((n_seq, POOL_HIST, POOL_WIDTH), F32)]
    return pl.pallas_call(
        kern,
        grid=(n_seq_tiles, n_row_tiles),
        in_specs=in_specs,
        out_specs=out_specs,
        out_shape=out_shape,
        scratch_shapes=[pltpu.VMEM((tn, POOL_HIST + tr, POOL_WIDTH), F32)],
        compiler_params=_cparams(("arbitrary", "arbitrary")),
        name="even_in",
    )(*args)


def _lambda_full(lam_ref, lam_init):
    lam = lam_ref[...]
    a = jnp.sum(lam[0:1, :] * lam[1:2, :], axis=1, keepdims=True)
    b = jnp.sum(lam[2:3, :] * lam[3:4, :], axis=1, keepdims=True)
    return jnp.exp(a) - jnp.exp(b) + lam_init


def _diff_finish(o1, o2, lam_full, lam_init, sub_w, gate):
    o = o1 - lam_full * o2
    ms = jnp.mean(o * o, axis=-1, keepdims=True)
    o = o * lax.rsqrt(ms + RMS_EPS) * sub_w * (1.0 - lam_init)
    return o * gate


def _stack_maps(q):
    lane = lax.broadcasted_iota(jnp.int32, q.shape, 1)
    zero = jnp.zeros_like(q)
    return jnp.concatenate([jnp.where(lane < DA_QK_DIM, q, zero), jnp.where(lane >= DA_QK_DIM, q, zero)], axis=0)


def _attn_prompt_kernel(q_ref, k_ref, v_ref, ga_ref, lam_ref, sub_ref, o_ref, m_ref, l_ref, acc_ref,
                        *, tq, lam_init):
    qi = pl.program_id(2)
    qs = _stack_maps(q_ref[...])
    m_ref[...] = jnp.full(m_ref.shape, -jnp.inf, F32)
    l_ref[...] = jnp.zeros(l_ref.shape, F32)
    acc_ref[...] = jnp.zeros(acc_ref.shape, F32)

    def step(ki, diagonal):
        start = pl.multiple_of(ki * tq, tq)
        kb = k_ref[pl.ds(start, tq), :]
        vb = v_ref[pl.ds(start, tq), :]
        s = _dot_nt(qs, kb)
        if diagonal:
            row = lax.broadcasted_iota(jnp.int32, s.shape, 0)
            row = jnp.where(row >= tq, row - tq, row)
            col = lax.broadcasted_iota(jnp.int32, s.shape, 1)
            s = jnp.where(col <= row, s, -jnp.inf)
        m_prev = m_ref[...]
        m_new = jnp.maximum(m_prev, jnp.max(s, axis=1, keepdims=True))
        alpha = jnp.exp(m_prev - m_new)
        p = jnp.exp(s - m_new)
        l_ref[...] = alpha * l_ref[...] + jnp.sum(p, axis=1, keepdims=True)
        acc_ref[...] = alpha * acc_ref[...] + _dot(p.astype(BF16), vb)
        m_ref[...] = m_new

    def body(ki, carry):
        step(ki, False)
        return carry

    lax.fori_loop(0, qi, body, 0)
    step(qi, True)

    acc = acc_ref[...]
    l = l_ref[...]
    o1 = acc[:tq] / l[:tq]
    o2 = acc[tq:] / l[tq:]
    out = _diff_finish(o1, o2, _lambda_full(lam_ref, lam_init), lam_init, sub_ref[...], ga_ref[...].astype(F32))
    o_ref[...] = out.astype(o_ref.dtype)


def _attn_prompt(q, kbf, vbf, ga, lam, sub_w, n_seq, t_seq, tq, lam_init):
    nq = t_seq // tq
    kern = functools.partial(_attn_prompt_kernel, tq=tq, lam_init=lam_init)
    qmap = lambda b, h, i: (b * nq + i, h)
    kvmap = lambda b, h, i: (b, h)
    return pl.pallas_call(
        kern,
        grid=(n_seq, DA_HEADS, nq),
        in_specs=[
            pl.BlockSpec((tq, LANES), qmap),
            pl.BlockSpec((t_seq, LANES), kvmap),
            pl.BlockSpec((t_seq, LANES), kvmap),
            pl.BlockSpec((tq, LANES), qmap),
            pl.BlockSpec(lam.shape, lambda b, h, i: (0, 0)),
            pl.BlockSpec(sub_w.shape, lambda b, h, i: (0, 0)),
        ],
        out_specs=pl.BlockSpec((tq, LANES), qmap),
        out_shape=jax.ShapeDtypeStruct((n_seq * t_seq, DA_WIDTH), BF16),
        scratch_shapes=[pltpu.VMEM((2 * tq, 1), F32), pltpu.VMEM((2 * tq, 1), F32),
                        pltpu.VMEM((2 * tq, DA_V_DIM), F32)],
        compiler_params=_cparams(("arbitrary", "arbitrary", "arbitrary")),
        name="attn_prompt",
    )(q, kbf, vbf, ga, lam, sub_w)


def _attn_sample_kernel(pt_ref, q_ref, ck_ref, cv_ref, kn_ref, vn_ref, ga_ref, lam_ref, sub_ref, o_ref,
                        wq_ref, m_ref, l_ref, acc_ref, *, t_new, lam_init):
    del pt_ref
    p_idx = pl.program_id(1)
    n_rows = 2 * DA_HEADS * t_new
    tshift = t_new.bit_length() - 1
    hmask = DA_HEADS - 1

    @pl.when(p_idx == 0)
    def _():
        for h in range(DA_HEADS):
            qh = q_ref[:, h * LANES:(h + 1) * LANES]
            st = _stack_maps(qh)
            wq_ref[h * t_new:(h + 1) * t_new, :] = st[:t_new]
            wq_ref[(DA_HEADS + h) * t_new:(DA_HEADS + h + 1) * t_new, :] = st[t_new:]
        m_ref[...] = jnp.full(m_ref.shape, -jnp.inf, F32)
        l_ref[...] = jnp.zeros(l_ref.shape, F32)
        acc_ref[...] = jnp.zeros(acc_ref.shape, F32)

    wq = wq_ref[...].astype(BF16)

    def update(s, vb):
        m_prev = m_ref[...]
        m_new = jnp.maximum(m_prev, jnp.max(s, axis=1, keepdims=True))
        alpha = jnp.exp(m_prev - m_new)
        p = jnp.exp(s - m_new)
        l_ref[...] = alpha * l_ref[...] + jnp.sum(p, axis=1, keepdims=True)
        acc_ref[...] = alpha * acc_ref[...] + _dot(p.astype(BF16), vb)
        m_ref[...] = m_new

    kb = ck_ref[...].astype(BF16)
    vb = cv_ref[...].astype(BF16)
    s = _dot_nt(wq, kb)
    row_head = (lax.broadcasted_iota(jnp.int32, s.shape, 0) >> tshift) & hmask
    col_head = lax.broadcasted_iota(jnp.int32, s.shape, 1) & hmask
    update(jnp.where(row_head == col_head, s, -jnp.inf), vb)

    @pl.when(p_idx == pl.num_programs(1) - 1)
    def _():
        kn = jnp.concatenate([kn_ref[:, h * LANES:(h + 1) * LANES] for h in range(DA_HEADS)], axis=0)
        vn = jnp.concatenate([vn_ref[:, h * LANES:(h + 1) * LANES] for h in range(DA_HEADS)], axis=0)
        sn = _dot_nt(wq, kn.astype(BF16))
        row = lax.broadcasted_iota(jnp.int32, sn.shape, 0)
        col = lax.broadcasted_iota(jnp.int32, sn.shape, 1)
        ok = (((row >> tshift) & hmask) == (col >> tshift)) & ((col & (t_new - 1)) <= (row & (t_new - 1)))
        update(jnp.where(ok, sn, -jnp.inf), vn.astype(BF16))

        acc = acc_ref[...]
        l = l_ref[...]
        half = n_rows // 2
        o1 = acc[:half] / l[:half]
        o2 = acc[half:] / l[half:]
        gate = jnp.concatenate([ga_ref[:, h * LANES:(h + 1) * LANES] for h in range(DA_HEADS)], axis=0)
        out = _diff_finish(o1, o2, _lambda_full(lam_ref, lam_init), lam_init, sub_ref[...], gate)
        for h in range(DA_HEADS):
            o_ref[:, h * LANES:(h + 1) * LANES] = out[h * t_new:(h + 1) * t_new].astype(o_ref.dtype)


def _attn_sample(page_table_flat, q, ck, cv, k_new, v_new, ga, lam, sub_w, layer_j, n_seq, t_new, n_pages,
                 lam_init):
    page_rows = ck.shape[2]
    kern = functools.partial(_attn_sample_kernel, t_new=t_new, lam_init=lam_init)
    seq_map = lambda n, p, pt: (n, 0)
    page_map = lambda n, p, pt: (pt[n * n_pages + p], layer_j, 0, 0)
    const2 = lambda n, p, pt: (0, 0)
    n_rows = 2 * DA_HEADS * t_new
    grid_spec = pltpu.PrefetchScalarGridSpec(
        num_scalar_prefetch=1,
        grid=(n_seq, n_pages),
        in_specs=[
            pl.BlockSpec((t_new, DA_WIDTH), seq_map),
            pl.BlockSpec((None, None, page_rows, LANES), page_map),
            pl.BlockSpec((None, None, page_rows, LANES), page_map),
            pl.BlockSpec((t_new, DA_WIDTH), seq_map),
            pl.BlockSpec((t_new, DA_WIDTH), seq_map),
            pl.BlockSpec((t_new, DA_WIDTH), seq_map),
            pl.BlockSpec(lam.shape, const2),
            pl.BlockSpec(sub_w.shape, const2),
        ],
        out_specs=pl.BlockSpec((t_new, DA_WIDTH), seq_map),
        scratch_shapes=[pltpu.VMEM((n_rows, LANES), F32), pltpu.VMEM((n_rows, 1), F32),
                        pltpu.VMEM((n_rows, 1), F32), pltpu.VMEM((n_rows, DA_V_DIM), F32)],
    )
    return pl.pallas_call(
        kern,
        grid_spec=grid_spec,
        out_shape=jax.ShapeDtypeStruct((n_seq * t_new, DA_WIDTH), F32),
        compiler_params=_cparams(("arbitrary", "arbitrary")),
        name="attn_sample",
    )(page_table_flat, q, ck, cv, k_new, v_new, ga, lam, sub_w)


def _out_ln_kernel(a_ref, b_ref, x_ref, w_ref, g_ref, bb_ref, y_ref):
    half = a_ref.shape[1]
    acc = _dot(a_ref[...].astype(BF16), w_ref[0:half, :]) + _dot(b_ref[...].astype(BF16), w_ref[half:2 * half, :])
    y_ref[...] = _layer_norm_rows(DN_ALPHA * x_ref[...] + acc, g_ref[...], bb_ref[...])


def _out_ln(a, b, x2d, w_out, ln_g, ln_b, tm):
    total = x2d.shape[0]
    row = lambda i: (i, 0)
    const = lambda i: (0, 0)
    return pl.pallas_call(
        _out_ln_kernel,
        grid=(total // tm,),
        in_specs=[
            pl.BlockSpec((tm, a.shape[1]), row),
            pl.BlockSpec((tm, b.shape[1]), row),
            pl.BlockSpec((tm, D_MODEL), row),
            pl.BlockSpec(w_out.shape, const),
            pl.BlockSpec(ln_g.shape, const),
            pl.BlockSpec(ln_b.shape, const),
        ],
        out_specs=pl.BlockSpec((tm, D_MODEL), row),
        out_shape=jax.ShapeDtypeStruct((total, D_MODEL), F32),
        compiler_params=_cparams(("arbitrary",)),
        name="out_ln",
    )(a, b, x2d, w_out, ln_g, ln_b)


def _gla_constants(top):
    big = GLA_BLOCK
    idx = np.arange(big)
    levels = int(round(math.log2(top)))
    seg = [((idx[:, None] // top == idx[None, :] // top) & (idx[None, :] <= idx[:, None])).astype(np.float32)]
    ups, sames = [], []
    n = top
    for _ in range(levels):
        start = (idx // n) * n
        rho = start + n // 2 - 1
        upper = idx > rho
        t = idx[None, :]
        c = np.where(upper[:, None], (t > rho[:, None]) & (t <= idx[:, None]), (t > idx[:, None]) & (t <= rho[:, None]))
        seg.append(c.astype(np.float32))
        ups.append(np.repeat(upper.astype(np.float32)[:, None], LANES, axis=1))
        sames.append((start[:, None] == start[None, :]).astype(np.float32))
        n //= 2
    return (jnp.asarray(np.concatenate(seg, axis=0), BF16), jnp.asarray(np.stack(ups)), jnp.asarray(np.stack(sames)),
            levels)


def _gla_intra(q_h, k_h, seg_h, up_ref, same_ref, levels):
    big = GLA_BLOCK
    att = jnp.zeros((big, big), F32)
    for lv in range(levels):
        e = jnp.exp(seg_h[(1 + lv) * big:(2 + lv) * big, :])
        up = up_ref[lv]
        qh = (q_h * e * up).astype(BF16)
        kh = (k_h * e * (1.0 - up)).astype(BF16)
        att = att + _dot_nt(qh, kh) * same_ref[lv]
    diag = jnp.sum(q_h * k_h, axis=1, keepdims=True)
    row = lax.broadcasted_iota(jnp.int32, (big, big), 0)
    col = lax.broadcasted_iota(jnp.int32, (big, big), 1)
    return att + jnp.where(row == col, diag, 0.0)


def _odd_kernel(*refs, tn, tr, levels, has_state):
    if has_state:
        (x_ref, w_ref, wga_ref, wgb_ref, bg_ref, nw_ref, wo_ref, lng_ref, lnb_ref, seg_ref, up_ref, same_ref,
         s0_ref, y_ref, sout_ref, q_s, k_s, v_s, g_s, o_s, st_s) = refs
    else:
        (x_ref, w_ref, wga_ref, wgb_ref, bg_ref, nw_ref, wo_ref, lng_ref, lnb_ref, seg_ref, up_ref, same_ref,
         y_ref, sout_ref, q_s, k_s, v_s, g_s, o_s, st_s) = refs
        s0_ref = None
    r = pl.program_id(1)
    rows = tn * tr
    big = GLA_BLOCK
    kw = GLA_HEADS * GLA_K_DIM
    vw = GLA_HEADS * GLA_V_DIM
    x = x_ref[...]
    xb = x.astype(BF16)
    q_s[...] = _dot(xb, w_ref[:, 0:kw]) * (GLA_K_DIM ** -0.5)
    k_s[...] = _dot(xb, w_ref[:, kw:2 * kw])
    v_s[...] = _dot(xb, w_ref[:, 2 * kw:2 * kw + vw]).astype(BF16)
    low = _dot(xb, wga_ref[...])
    z = _dot(low.astype(BF16), wgb_ref[...]) + bg_ref[...]
    g_s[...] = (jnp.minimum(z, 0.0) - jnp.log1p(jnp.exp(-jnp.abs(z)))) / GLA_TAU

    if not has_state:
        @pl.when(r == 0)
        def _():
            st_s[...] = jnp.zeros(st_s.shape, F32)

    def block(c, carry):
        r0 = pl.multiple_of(c * big, big)
        qb = q_s[pl.ds(r0, big), :]
        kb = k_s[pl.ds(r0, big), :]
        gb = g_s[pl.ds(r0, big), :]
        hi = gb.astype(BF16)
        lo = (gb - hi.astype(F32)).astype(BF16)
        seg2 = _dot(seg_ref[...], jnp.concatenate([hi, lo], axis=1))
        seg = seg2[:, :kw] + seg2[:, kw:]
        for h in range(GLA_HEADS):
            ks = slice(h * GLA_K_DIM, (h + 1) * GLA_K_DIM)
            q_h, k_h, seg_h = qb[:, ks], kb[:, ks], seg[:, ks]
            v_h = v_s[pl.ds(r0, big), h * GLA_V_DIM:(h + 1) * GLA_V_DIM]
            att = _gla_intra(q_h, k_h, seg_h, up_ref, same_ref, levels)
            o_h = _dot(att.astype(BF16), v_h)
            cb = seg_h[0:big, :]
            qe = q_h * jnp.exp(cb)
            if has_state:
                n_runs = big // tr
                v_hf = v_h.astype(F32)
                pad_k = jnp.zeros((2 * SUBLANES - tr, GLA_K_DIM), F32)
                pad_v = jnp.zeros((2 * SUBLANES - tr, GLA_V_DIM), F32)
                inter = []
                for s in range(n_runs):
                    rs = slice(s * tr, (s + 1) * tr)
                    seq = c * n_runs + s
                    st = s0_ref[seq, h].T
                    qep = jnp.concatenate([qe[rs], pad_k], axis=0).astype(BF16)
                    inter.append(_dot_nt(qep, st.astype(BF16))[0:tr])
                    cb_last = cb[(s + 1) * tr - 1:(s + 1) * tr, :]
                    kt = k_h[rs] * jnp.exp(cb_last - cb[rs])
                    ktp = jnp.concatenate([kt, pad_k], axis=0).astype(BF16)
                    vtp = jnp.concatenate([v_hf[rs], pad_v], axis=0).astype(BF16)
                    st_new = st * jnp.exp(cb_last) + _dot_tn(vtp, ktp)
                    sout_ref[seq, h] = st_new.T
                o_h = o_h + jnp.concatenate(inter, axis=0)
            else:
                st = st_s[h]
                o_h = o_h + _dot_nt(qe.astype(BF16), st.astype(BF16))
                cb_last = cb[big - 1:big, :]
                kt = (k_h * jnp.exp(cb_last - cb)).astype(BF16)
                st_s[h] = st * jnp.exp(cb_last) + _dot_tn(v_h, kt)
            o_s[pl.ds(r0, big), h * GLA_V_DIM:(h + 1) * GLA_V_DIM] = o_h
        return carry

    lax.fori_loop(0, rows // big, block, 0)

    if not has_state:
        @pl.when(r == pl.num_programs(1) - 1)
        def _():
            for h in range(GLA_HEADS):
                sout_ref[0, h] = st_s[h].T

    nw = nw_ref[...]
    for h in range(GLA_HEADS):
        vs = slice(h * GLA_V_DIM, (h + 1) * GLA_V_DIM)
        o_h = o_s[:, vs]
        ms = jnp.mean(o_h * o_h, axis=-1, keepdims=True)
        gate = _dot(xb, w_ref[:, 2 * kw + vw + h * GLA_V_DIM:2 * kw + vw + (h + 1) * GLA_V_DIM])
        o_s[:, vs] = o_h * lax.rsqrt(ms + RMS_EPS) * nw * _silu(gate)
    acc = _dot(o_s[...].astype(BF16), wo_ref[...])
    y_ref[...] = _layer_norm_rows(DN_ALPHA * x + acc, lng_ref[...], lnb_ref[...])


def _odd_layer(x2d, n_seq, t_seq, tn, tr, w_in, wga, wgb, bg, nw, wo, lng, lnb, state0, layer_j):
    has_state = state0 is not None
    seg, up, same, levels = _gla_constants(tr if has_state else GLA_BLOCK)
    n_row_tiles = t_seq // tr
    rows = tn * tr
    total = n_seq * t_seq
    kern = functools.partial(_odd_kernel, tn=tn, tr=tr, levels=levels, has_state=has_state)
    row_map = lambda n, r: (n * n_row_tiles + r, 0)
    c2 = lambda n, r: (0, 0)
    c3 = lambda n, r: (0, 0, 0)
    st_map = lambda n, r: (n, 0, 0, 0)
    in_specs = [
        pl.BlockSpec((rows, D_MODEL), row_map),
        pl.BlockSpec(w_in.shape, c2), pl.BlockSpec(wga.shape, c2), pl.BlockSpec(wgb.shape, c2),
        pl.BlockSpec(bg.shape, c2), pl.BlockSpec(nw.shape, c2), pl.BlockSpec(wo.shape, c2),
        pl.BlockSpec(lng.shape, c2), pl.BlockSpec(lnb.shape, c2),
        pl.BlockSpec(seg.shape, c2), pl.BlockSpec(up.shape, c3), pl.BlockSpec(same.shape, c3),
    ]
    args = [x2d, w_in, wga, wgb, bg, nw, wo, lng, lnb, seg, up, same]
    st_block = (tn, GLA_HEADS, GLA_K_DIM, GLA_V_DIM)
    if has_state:
        in_specs.append(pl.BlockSpec((tn, None, GLA_HEADS, GLA_K_DIM, GLA_V_DIM),
                                     lambda n, r: (n, layer_j, 0, 0, 0)))
        args.append(state0)
    kw = GLA_HEADS * GLA_K_DIM
    vw = GLA_HEADS * GLA_V_DIM
    return pl.pallas_call(
        kern,
        grid=(n_seq // tn, n_row_tiles),
        in_specs=in_specs,
        out_specs=[pl.BlockSpec((rows, D_MODEL), row_map), pl.BlockSpec(st_block, st_map)],
        out_shape=[jax.ShapeDtypeStruct((total, D_MODEL), F32),
                   jax.ShapeDtypeStruct((n_seq, GLA_HEADS, GLA_K_DIM, GLA_V_DIM), F32)],
        scratch_shapes=[pltpu.VMEM((rows, kw), F32), pltpu.VMEM((rows, kw), F32), pltpu.VMEM((rows, vw), BF16),
                        pltpu.VMEM((rows, kw), F32), pltpu.VMEM((rows, vw), F32),
                        pltpu.VMEM((GLA_HEADS, GLA_V_DIM, GLA_K_DIM), F32)],
        compiler_params=_cparams(("arbitrary", "arbitrary")),
        name="odd_layer",
    )(*args)


def _rope_tables(positions):
    half = ROT_DIM // 2
    inv_freq = jnp.power(jnp.float32(ROPE_THETA), -jnp.arange(0, ROT_DIM, 2, dtype=jnp.float32) / ROT_DIM)
    ang = positions.astype(jnp.float32)[:, None] * inv_freq[None, :]
    cos, sin = jnp.cos(ang), jnp.sin(ang)
    n = positions.shape[0]
    ones = jnp.ones((n, DA_QK_DIM - ROT_DIM), F32)
    zeros = jnp.zeros((n, DA_QK_DIM - ROT_DIM), F32)
    zh = jnp.zeros((n, half), F32)
    cos_map = jnp.concatenate([cos, cos, ones], axis=1)
    sina_map = jnp.concatenate([zh, sin, zeros], axis=1)
    sinb_map = jnp.concatenate([-sin, zh, zeros], axis=1)
    two = lambda t: jnp.concatenate([t, t], axis=1)
    return two(cos_map), two(sina_map), two(sinb_map)


def kernel(x_prompt, x_sample, cache_k, cache_v, state_pool, state_gla, page_table, w_in_even, w_pool_lin,
           pool_scale, diff_lambda_params, subln_w, w_out_even, w_in_odd, w_gate_a, w_gate_b, b_gate, gla_norm_w,
           w_out_odd, ln_g, ln_b):
    n_p, t_p, _ = x_prompt.shape
    n_s, t_s, _ = x_sample.shape
    n_pool, n_even, page_size = cache_k.shape[:3]
    n_pages = page_table.shape[1]
    past_len = n_pages * page_size

    tr_p = min(512, t_p)
    tq_p = min(256, t_p)
    tn_even_s = 16
    tn_odd_s = GLA_BLOCK // t_s

    w_in_even_b = w_in_even.astype(BF16)
    w_pool_b = w_pool_lin.astype(BF16)
    w_out_even_b = w_out_even.astype(BF16)
    w_in_odd_b = w_in_odd.astype(BF16)
    w_out_odd_b = w_out_odd.astype(BF16)
    pad_rank = LANES - GLA_GATE_RANK
    w_ga_b = jnp.pad(w_gate_a, ((0, 0), (0, 0), (0, pad_rank))).astype(BF16)
    w_gb_b = jnp.pad(w_gate_b, ((0, 0), (0, pad_rank), (0, 0))).astype(BF16)

    cos_p, sina_p, sinb_p = _rope_tables(jnp.arange(t_p))
    cos_s, sina_s, sinb_s = (jnp.tile(t, (tn_even_s, 1)) for t in _rope_tables(past_len + jnp.arange(t_s)))

    ck = cache_k.reshape(n_pool, n_even, page_size * DA_HEADS, 2 * DA_QK_DIM)
    cv = cache_v.reshape(n_pool, n_even, page_size * DA_HEADS, DA_V_DIM)
    pt_flat = page_table.reshape(-1).astype(jnp.int32)
    pool16 = jnp.pad(state_pool, ((0, 0), (0, 0), (POOL_HIST - state_pool.shape[2], 0), (0, 0)))

    yp = x_prompt.reshape(n_p * t_p, D_MODEL)
    ys = x_sample.reshape(n_s * t_s, D_MODEL)
    kp, vp, pp, gp, ksm, vsm, psm, gsm = [], [], [], [], [], [], [], []
    for i in range(DEPTH):
        j = i // 2
        lng, lnb = ln_g[i][None, :], ln_b[i][None, :]
        if i % 2 == 0:
            lam_init = 0.8 - 0.6 * math.exp(-0.3 * i)
            pscale = pool_scale[j][None, :]
            sub_w = subln_w[j][None, :]
            lam = diff_lambda_params[j]
            pmix, q, k32, kbf, v32, vbf, ga, npool = _even_in(
                yp, n_p, t_p, 1, tr_p, 0, w_in_even_b[j], w_pool_b[j], pscale, cos_p, sina_p, sinb_p, None, j, BF16)
            omix = _attn_prompt(q, kbf, vbf, ga, lam, sub_w, n_p, t_p, tq_p, lam_init)
            yp = _out_ln(pmix, omix, yp, w_out_even_b[j], lng, lnb, tr_p)
            kp.append(k32.reshape(n_p, t_p, DA_HEADS, 2 * DA_QK_DIM))
            vp.append(v32.reshape(n_p, t_p, DA_HEADS, DA_V_DIM))
            pp.append(npool[:, 1:])
            pmix, q, k32, _, v32, _, ga, npool = _even_in(
                ys, n_s, t_s, tn_even_s, t_s, past_len, w_in_even_b[j], w_pool_b[j], pscale, cos_s, sina_s, sinb_s,
                pool16, j, F32)
            omix = _attn_sample(pt_flat, q, ck, cv, k32, v32, ga, lam, sub_w, j, n_s, t_s, n_pages, lam_init)
            ys = _out_ln(pmix, omix, ys, w_out_even_b[j], lng, lnb, min(512, n_s * t_s))
            ksm.append(k32.reshape(n_s, t_s, DA_HEADS, 2 * DA_QK_DIM))
            vsm.append(v32.reshape(n_s, t_s, DA_HEADS, DA_V_DIM))
            psm.append(npool[:, 1:])
        else:
            ow = (w_in_odd_b[j], w_ga_b[j], w_gb_b[j], b_gate[j][None, :], gla_norm_w[j][None, :], w_out_odd_b[j],
                  lng, lnb)
            yp, s_p = _odd_layer(yp, n_p, t_p, 1, tr_p, *ow, None, j)
            ys, s_s = _odd_layer(ys, n_s, t_s, tn_odd_s, t_s, *ow, state_gla, j)
            gp.append(s_p)
            gsm.append(s_s)
    return (yp.reshape(n_p, t_p, D_MODEL), ys.reshape(n_s, t_s, D_MODEL),
            jnp.stack(kp, axis=1), jnp.stack(vp, axis=1), jnp.stack(pp, axis=1), jnp.stack(gp, axis=1),
            jnp.stack(ksm, axis=1), jnp.stack(vsm, axis=1), jnp.stack(psm, axis=1), jnp.stack(gsm, axis=1))
```

```python
import functools
import math

import numpy as np
import jax
import jax.numpy as jnp
from jax import lax
from jax.experimental import pallas as pl
from jax.experimental.pallas import tpu as pltpu

F32 = jnp.float32
BF16 = jnp.bfloat16

LANES = 128
SUBLANES = 8
VMEM_LIMIT_BYTES = 56 * 1024 * 1024

D_MODEL = 1024
DEPTH = 4
POOL_WIDTH = 512
POOL_WINDOWS = (2, 4, 8, 16)
POOL_GROUP_DIM = 128
POOL_HIST = 16
DA_HEADS = 4
DA_QK_DIM = 64
DA_V_DIM = 128
DA_WIDTH = 512
ROT_DIM = 16
ROPE_THETA = 500000.0
GLA_HEADS = 4
GLA_K_DIM = 128
GLA_V_DIM = 256
GLA_GATE_RANK = 16
GLA_TAU = 16.0
GLA_BLOCK = 64
DN_ALPHA = (2 * DEPTH) ** 0.25
LN_EPS = 1e-5
RMS_EPS = 1e-5

_NT = (((1,), (1,)), ((), ()))
_TN = (((0,), (0,)), ((), ()))


def _dot(a, b):
    return jnp.dot(a, b, preferred_element_type=F32)


def _dot_nt(a, b):
    return lax.dot_general(a, b, _NT, preferred_element_type=F32)


def _dot_tn(a, b):
    return lax.dot_general(a, b, _TN, preferred_element_type=F32)


def _silu(x):
    return x * jax.nn.sigmoid(x)


def _cparams(sem):
    return pltpu.CompilerParams(dimension_semantics=sem, vmem_limit_bytes=VMEM_LIMIT_BYTES)


def _layer_norm_rows(z, g, b):
    mu = jnp.mean(z, axis=-1, keepdims=True)
    zc = z - mu
    var = jnp.mean(zc * zc, axis=-1, keepdims=True)
    return zc * lax.rsqrt(var + LN_EPS) * g + b


def _even_in_kernel(*refs, tn, tr, tk, pos0, has_past):
    if has_past:
        (x_ref, w_ref, wlin_ref, pscale_ref, cos_ref, sina_ref, sinb_ref, buf_ref,
         pmix_ref, q_ref, k32_ref, v32_ref, ga_ref, npool_ref, ext_ref) = refs
        wvt_ref = kbf_ref = vt_ref = None
    else:
        (x_ref, w_ref, wlin_ref, pscale_ref, cos_ref, sina_ref, sinb_ref, wvt_ref,
         pmix_ref, q_ref, k32_ref, v32_ref, ga_ref, npool_ref, kbf_ref, vt_ref, ext_ref) = refs
        buf_ref = None
    r = pl.program_id(1)
    rows = tn * tr
    xb = x_ref[...].astype(BF16)

    def proj(c0, n):
        return _dot(xb, w_ref[:, c0:c0 + n])

    pu = proj(0, POOL_WIDTH)
    pg = proj(POOL_WIDTH, POOL_WIDTH)
    if has_past:
        ext_ref[:, 0:POOL_HIST, :] = buf_ref[...]
    else:
        @pl.when(r == 0)
        def _():
            ext_ref[:, 0:POOL_HIST, :] = jnp.zeros((tn, POOL_HIST, POOL_WIDTH), F32)

        @pl.when(r > 0)
        def _():
            ext_ref[:, 0:POOL_HIST, :] = ext_ref[:, tr:tr + POOL_HIST, :]
    for s in range(tn):
        ext_ref[s, POOL_HIST:POOL_HIST + tr, :] = pu[s * tr:(s + 1) * tr, :]
    npool_ref[...] = ext_ref[:, tr:tr + POOL_HIST, :]

    pos = pos0 + r * tr + lax.broadcasted_iota(jnp.int32, (tr, POOL_GROUP_DIM), 0)
    for g, w in enumerate(POOL_WINDOWS):
        c0 = g * POOL_GROUP_DIM
        cnt = jnp.minimum(pos + 1, w).astype(F32)
        pieces = []
        for s in range(tn):
            u_g = ext_ref[s, POOL_HIST:POOL_HIST + tr, c0:c0 + POOL_GROUP_DIM]
            acc = u_g
            for i in range(1, w):
                acc = acc + ext_ref[s, POOL_HIST - i:POOL_HIST - i + tr, c0:c0 + POOL_GROUP_DIM]
            pieces.append(acc / cnt - u_g)
        d = pieces[0] if tn == 1 else jnp.concatenate(pieces, axis=0)
        y = _dot(d.astype(BF16), wlin_ref[g]) * pscale_ref[:, c0:c0 + POOL_GROUP_DIM]
        pmix_ref[:, c0:c0 + POOL_GROUP_DIM] = (y * _silu(pg[:, c0:c0 + POOL_GROUP_DIM])).astype(pmix_ref.dtype)

    cos = cos_ref[...]
    sina = sina_ref[...]
    sinb = sinb_ref[...]

    def rope(xh):
        return (xh * cos + pltpu.roll(xh, ROT_DIM // 2, 1) * sina
                + pltpu.roll(xh, LANES - ROT_DIM // 2, 1) * sinb)

    q = proj(2 * POOL_WIDTH, DA_WIDTH)
    k = proj(2 * POOL_WIDTH + DA_WIDTH, DA_WIDTH)
    for h in range(DA_HEADS):
        sl = slice(h * LANES, (h + 1) * LANES)
        q_ref[:, sl] = (rope(q[:, sl]) * (DA_QK_DIM ** -0.5)).astype(q_ref.dtype)
        kr = rope(k[:, sl])
        k32_ref[:, sl] = kr
        if kbf_ref is not None:
            kbf_ref[:, sl] = kr.astype(kbf_ref.dtype)
    v32_ref[...] = proj(2 * POOL_WIDTH + 2 * DA_WIDTH, DA_WIDTH)
    if vt_ref is not None:
        vt = _dot_nt(wvt_ref[...], xb).astype(vt_ref.dtype)
        for h in range(DA_HEADS):
            for c in range(tr // tk):
                vt_ref[h, c] = vt[h * DA_V_DIM:(h + 1) * DA_V_DIM, c * tk:(c + 1) * tk]
    ga_ref[...] = _silu(proj(2 * POOL_WIDTH + 3 * DA_WIDTH, DA_WIDTH)).astype(ga_ref.dtype)


def _even_in(x2d, n_seq, t_seq, tn, tr, tk, pos0, w_in, wlin, pscale, cos, sina, sinb, w_vt, buf16, layer_j,
             act_dtype):
    n_row_tiles = t_seq // tr
    n_seq_tiles = n_seq // tn
    rows = tn * tr
    total = n_seq * t_seq
    has_past = buf16 is not None
    kern = functools.partial(_even_in_kernel, tn=tn, tr=tr, tk=tk, pos0=pos0, has_past=has_past)
    row_map = lambda n, r: (n * n_row_tiles + r, 0)
    const2 = lambda n, r: (0, 0)
    tab_map = (lambda n, r: (r, 0)) if not has_past else const2
    in_specs = [
        pl.BlockSpec((rows, D_MODEL), row_map),
        pl.BlockSpec(w_in.shape, const2),
        pl.BlockSpec(wlin.shape, lambda n, r: (0, 0, 0)),
        pl.BlockSpec(pscale.shape, const2),
        pl.BlockSpec((rows, LANES), tab_map),
        pl.BlockSpec((rows, LANES), tab_map),
        pl.BlockSpec((rows, LANES), tab_map),
    ]
    args = [x2d, w_in, wlin, pscale, cos, sina, sinb]
    blk = lambda: pl.BlockSpec((rows, DA_WIDTH), row_map)
    sds = lambda dt: jax.ShapeDtypeStruct((total, DA_WIDTH), dt)
    out_specs = [blk() for _ in range(5)] + [pl.BlockSpec((tn, POOL_HIST, POOL_WIDTH), lambda n, r: (n, 0, 0))]
    out_shape = [sds(act_dtype), sds(act_dtype), sds(F32), sds(F32), sds(act_dtype),
                 jax.ShapeDtypeStruct((n_seq, POOL_HIST, POOL_WIDTH), F32)]
    if has_past:
        in_specs.append(pl.BlockSpec((tn, None, POOL_HIST, POOL_WIDTH), lambda n, r: (n, layer_j, 0, 0)))
        args.append(buf16)
    else:
        in_specs.append(pl.BlockSpec(w_vt.shape, const2))
        args.append(w_vt)
        out_specs += [blk(), pl.BlockSpec((None, DA_HEADS, tr // tk, DA_V_DIM, tk), lambda n, r: (n, 0, r, 0, 0))]
        out_shape += [sds(BF16), jax.ShapeDtypeStruct((n_seq, DA_HEADS, t_seq // tk, DA_V_DIM, tk), BF16)]
    return pl.pallas_call(
        kern,
        grid=(n_seq_tiles, n_row_tiles),
        in_specs=in_specs,
        out_specs=out_specs,
        out_shape=out_shape,
        scratch_shapes=[pltpu.VMEM((tn, POOL_HIST + tr, POOL_WIDTH), F32)],
        compiler_params=_cparams(("arbitrary", "arbitrary")),
        name="even_in",
    )(*args)


def _lambda_full(lam_ref, lam_init):
    lam = lam_ref[...]
    a = jnp.sum(lam[0:1, :] * lam[1:2, :], axis=1, keepdims=True)
    b = jnp.sum(lam[2:3, :] * lam[3:4, :], axis=1, keepdims=True)
    return jnp.exp(a) - jnp.exp(b) + lam_init


def _diff_finish(o1, o2, lam_full, lam_init, sub_w, gate):
    o = o1 - lam_full * o2
    ms = jnp.mean(o * o, axis=-1, keepdims=True)
    o = o * lax.rsqrt(ms + RMS_EPS) * sub_w * (1.0 - lam_init)
    return o * gate


def _stack_maps(q):
    lane = lax.broadcasted_iota(jnp.int32, q.shape, 1)
    zero = jnp.zeros_like(q)
    return jnp.concatenate([jnp.where(lane < DA_QK_DIM, q, zero), jnp.where(lane >= DA_QK_DIM, q, zero)], axis=0)


def _attn_prompt_kernel(q_ref, k_ref, vt_ref, ga_ref, lam_ref, sub_ref, o_ref, acc_ref, *, tq, lam_init):
    qi = pl.program_id(2)
    qs = _stack_maps(q_ref[...])
    acc_ref[...] = jnp.zeros(acc_ref.shape, F32)

    def step(ki, m_prev, l_prev, diagonal):
        start = pl.multiple_of(ki * tq, tq)
        kb = k_ref[pl.ds(start, tq), :]
        s = _dot_nt(kb, qs)
        if diagonal:
            key = lax.broadcasted_iota(jnp.int32, s.shape, 0)
            qry = lax.broadcasted_iota(jnp.int32, s.shape, 1)
            qry = jnp.where(qry >= tq, qry - tq, qry)
            s = jnp.where(key <= qry, s, -jnp.inf)
        m_new = jnp.maximum(m_prev, jnp.max(s, axis=0, keepdims=True))
        alpha = jnp.exp(m_prev - m_new)
        p = jnp.exp(s - m_new)
        l_new = alpha * l_prev + jnp.sum(p, axis=0, keepdims=True)
        acc_ref[...] = alpha * acc_ref[...] + _dot(vt_ref[ki], p.astype(BF16))
        return m_new, l_new

    m0 = jnp.full((1, 2 * tq), -jnp.inf, F32)
    l0 = jnp.zeros((1, 2 * tq), F32)
    m, l = lax.fori_loop(0, qi, lambda ki, c: step(ki, c[0], c[1], False), (m0, l0))
    m, l = step(qi, m, l, True)

    o_t = acc_ref[...] / l
    o1 = o_t[:, :tq].T
    o2 = o_t[:, tq:].T
    out = _diff_finish(o1, o2, _lambda_full(lam_ref, lam_init), lam_init, sub_ref[...], ga_ref[...].astype(F32))
    o_ref[...] = out.astype(o_ref.dtype)


def _attn_prompt(q, kbf, vt, ga, lam, sub_w, n_seq, t_seq, tq, lam_init):
    nq = t_seq // tq
    kern = functools.partial(_attn_prompt_kernel, tq=tq, lam_init=lam_init)
    qmap = lambda b, h, i: (b * nq + i, h)
    c2 = lambda b, h, i: (0, 0)
    return pl.pallas_call(
        kern,
        grid=(n_seq, DA_HEADS, nq),
        in_specs=[
            pl.BlockSpec((tq, LANES), qmap),
            pl.BlockSpec((t_seq, LANES), lambda b, h, i: (b, h)),
            pl.BlockSpec((None, None, nq, DA_V_DIM, tq), lambda b, h, i: (b, h, 0, 0, 0)),
            pl.BlockSpec((tq, LANES), qmap),
            pl.BlockSpec(lam.shape, c2),
            pl.BlockSpec(sub_w.shape, c2),
        ],
        out_specs=pl.BlockSpec((tq, LANES), qmap),
        out_shape=jax.ShapeDtypeStruct((n_seq * t_seq, DA_WIDTH), BF16),
        scratch_shapes=[pltpu.VMEM((DA_V_DIM, 2 * tq), F32)],
        compiler_params=_cparams(("arbitrary", "arbitrary", "arbitrary")),
        name="attn_prompt",
    )(q, kbf, vt, ga, lam, sub_w)


def _attn_sample_kernel(*refs, t_new, n_pages, lam_init):
    pt_ref, q_ref = refs[0], refs[1]
    k_refs = refs[2:2 + n_pages]
    v_refs = refs[2 + n_pages:2 + 2 * n_pages]
    (kn_ref, vn_ref, ga_ref, lam_ref, sub_ref, bias_ref, biasn_ref, o_ref, s_ref) = refs[2 + 2 * n_pages:]
    del pt_ref
    n_rows = 2 * DA_HEADS * t_new

    pieces = [None] * (2 * DA_HEADS)
    for h in range(DA_HEADS):
        st = _stack_maps(q_ref[:, h * LANES:(h + 1) * LANES])
        pieces[h] = st[:t_new]
        pieces[DA_HEADS + h] = st[t_new:]
    wq = jnp.concatenate(pieces, axis=0).astype(BF16)

    bias = bias_ref[...]
    mx = None
    for p in range(n_pages):
        s = _dot_nt(wq, k_refs[p][...].astype(BF16)) + bias
        s_ref[p] = s
        mx = s if mx is None else jnp.maximum(mx, s)
    kn = jnp.concatenate([kn_ref[:, h * LANES:(h + 1) * LANES] for h in range(DA_HEADS)], axis=0)
    vn = jnp.concatenate([vn_ref[:, h * LANES:(h + 1) * LANES] for h in range(DA_HEADS)], axis=0)
    sn = _dot_nt(wq, kn.astype(BF16)) + biasn_ref[...]
    m = jnp.maximum(jnp.max(mx, axis=1, keepdims=True), jnp.max(sn, axis=1, keepdims=True))

    pn = jnp.exp(sn - m)
    acc = _dot(pn.astype(BF16), vn.astype(BF16))
    lsum = None
    for p in range(n_pages):
        pe = jnp.exp(s_ref[p] - m)
        lsum = pe if lsum is None else lsum + pe
        acc = acc + _dot(pe.astype(BF16), v_refs[p][...].astype(BF16))
    l = jnp.sum(lsum, axis=1, keepdims=True) + jnp.sum(pn, axis=1, keepdims=True)

    half = n_rows // 2
    o1 = acc[:half] / l[:half]
    o2 = acc[half:] / l[half:]
    gate = jnp.concatenate([ga_ref[:, h * LANES:(h + 1) * LANES] for h in range(DA_HEADS)], axis=0)
    out = _diff_finish(o1, o2, _lambda_full(lam_ref, lam_init), lam_init, sub_ref[...], gate)
    for h in range(DA_HEADS):
        o_ref[:, h * LANES:(h + 1) * LANES] = out[h * t_new:(h + 1) * t_new].astype(o_ref.dtype)


def _sample_biases(t_new, page_rows):
    n_rows = 2 * DA_HEADS * t_new
    row = np.arange(n_rows)[:, None]
    row_head, row_q = (row // t_new) % DA_HEADS, row % t_new
    col = np.arange(page_rows)[None, :]
    bias = np.where(row_head == col % DA_HEADS, 0.0, -np.inf).astype(np.float32)
    coln = np.arange(DA_HEADS * t_new)[None, :]
    biasn = np.where((row_head == coln // t_new) & (coln % t_new <= row_q), 0.0, -np.inf).astype(np.float32)
    return jnp.asarray(bias), jnp.asarray(biasn)


def _attn_sample(page_table_flat, q, ck, cv, k_new, v_new, ga, lam, sub_w, layer_j, n_seq, t_new, n_pages,
                 lam_init):
    page_rows = ck.shape[2]
    kern = functools.partial(_attn_sample_kernel, t_new=t_new, n_pages=n_pages, lam_init=lam_init)
    bias, biasn = _sample_biases(t_new, page_rows)
    seq_map = lambda n, pt: (n, 0)
    const2 = lambda n, pt: (0, 0)
    n_rows = 2 * DA_HEADS * t_new

    def page_spec(p):
        return pl.BlockSpec((None, None, page_rows, LANES), lambda n, pt: (pt[n * n_pages + p], layer_j, 0, 0))

    seq_spec = lambda: pl.BlockSpec((t_new, DA_WIDTH), seq_map)
    grid_spec = pltpu.PrefetchScalarGridSpec(
        num_scalar_prefetch=1,
        grid=(n_seq,),
        in_specs=([seq_spec()] + [page_spec(p) for p in range(n_pages)] + [page_spec(p) for p in range(n_pages)]
                  + [seq_spec(), seq_spec(), seq_spec(), pl.BlockSpec(lam.shape, const2),
                     pl.BlockSpec(sub_w.shape, const2), pl.BlockSpec(bias.shape, const2),
                     pl.BlockSpec(biasn.shape, const2)]),
        out_specs=seq_spec(),
        scratch_shapes=[pltpu.VMEM((n_pages, n_rows, page_rows), F32)],
    )
    return pl.pallas_call(
        kern,
        grid_spec=grid_spec,
        out_shape=jax.ShapeDtypeStruct((n_seq * t_new, DA_WIDTH), F32),
        compiler_params=_cparams(("arbitrary",)),
        name="attn_sample",
    )(page_table_flat, q, *([ck] * n_pages), *([cv] * n_pages), k_new, v_new, ga, lam, sub_w, bias, biasn)


def _out_ln_kernel(a_ref, b_ref, x_ref, w_ref, g_ref, bb_ref, y_ref):
    half = a_ref.shape[1]
    acc = _dot(a_ref[...].astype(BF16), w_ref[0:half, :]) + _dot(b_ref[...].astype(BF16), w_ref[half:2 * half, :])
    y_ref[...] = _layer_norm_rows(DN_ALPHA * x_ref[...] + acc, g_ref[...], bb_ref[...])


def _out_ln(a, b, x2d, w_out, ln_g, ln_b, tm):
    total = x2d.shape[0]
    row = lambda i: (i, 0)
    const = lambda i: (0, 0)
    return pl.pallas_call(
        _out_ln_kernel,
        grid=(total // tm,),
        in_specs=[
            pl.BlockSpec((tm, a.shape[1]), row),
            pl.BlockSpec((tm, b.shape[1]), row),
            pl.BlockSpec((tm, D_MODEL), row),
            pl.BlockSpec(w_out.shape, const),
            pl.BlockSpec(ln_g.shape, const),
            pl.BlockSpec(ln_b.shape, const),
        ],
        out_specs=pl.BlockSpec((tm, D_MODEL), row),
        out_shape=jax.ShapeDtypeStruct((total, D_MODEL), F32),
        compiler_params=_cparams(("arbitrary",)),
        name="out_ln",
    )(a, b, x2d, w_out, ln_g, ln_b)


def _gla_constants(top):
    big = GLA_BLOCK
    idx = np.arange(big)
    levels = int(round(math.log2(top)))
    seg = [((idx[:, None] // top == idx[None, :] // top) & (idx[None, :] <= idx[:, None])).astype(np.float32)]
    ups, sames = [], []
    n = top
    for _ in range(levels):
        start = (idx // n) * n
        rho = start + n // 2 - 1
        upper = idx > rho
        t = idx[None, :]
        c = np.where(upper[:, None], (t > rho[:, None]) & (t <= idx[:, None]), (t > idx[:, None]) & (t <= rho[:, None]))
        seg.append(c.astype(np.float32))
        ups.append(np.repeat(upper.astype(np.float32)[:, None], LANES, axis=1))
        sames.append((start[:, None] == start[None, :]).astype(np.float32))
        n //= 2
    return (jnp.asarray(np.concatenate(seg, axis=0), BF16), jnp.asarray(np.stack(ups)), jnp.asarray(np.stack(sames)),
            levels)


def _gla_intra(q_h, k_h, seg_h, up_ref, same_ref, levels):
    big = GLA_BLOCK
    att = jnp.zeros((big, big), F32)
    for lv in range(levels):
        e = jnp.exp(seg_h[(1 + lv) * big:(2 + lv) * big, :])
        up = up_ref[lv]
        qh = (q_h * e * up).astype(BF16)
        kh = (k_h * e * (1.0 - up)).astype(BF16)
        att = att + _dot_nt(qh, kh) * same_ref[lv]
    diag = jnp.sum(q_h * k_h, axis=1, keepdims=True)
    row = lax.broadcasted_iota(jnp.int32, (big, big), 0)
    col = lax.broadcasted_iota(jnp.int32, (big, big), 1)
    return att + jnp.where(row == col, diag, 0.0)


def _odd_kernel(*refs, tn, tr, levels, has_state):
    if has_state:
        (x_ref, w_ref, wga_ref, wgb_ref, bg_ref, nw_ref, wo_ref, lng_ref, lnb_ref, seg_ref, up_ref, same_ref,
         s0_ref, y_ref, sout_ref, q_s, k_s, v_s, g_s, o_s, st_s) = refs
    else:
        (x_ref, w_ref, wga_ref, wgb_ref, bg_ref, nw_ref, wo_ref, lng_ref, lnb_ref, seg_ref, up_ref, same_ref,
         y_ref, sout_ref, q_s, k_s, v_s, g_s, o_s, st_s) = refs
        s0_ref = None
    r = pl.program_id(1)
    rows = tn * tr
    big = GLA_BLOCK
    kw = GLA_HEADS * GLA_K_DIM
    vw = GLA_HEADS * GLA_V_DIM
    x = x_ref[...]
    xb = x.astype(BF16)
    q_s[...] = _dot(xb, w_ref[:, 0:kw]) * (GLA_K_DIM ** -0.5)
    k_s[...] = _dot(xb, w_ref[:, kw:2 * kw])
    v_s[...] = _dot(xb, w_ref[:, 2 * kw:2 * kw + vw]).astype(BF16)
    low = _dot(xb, wga_ref[...])
    z = _dot(low.astype(BF16), wgb_ref[...]) + bg_ref[...]
    g_s[...] = (jnp.minimum(z, 0.0) - jnp.log1p(jnp.exp(-jnp.abs(z)))) / GLA_TAU

    if not has_state:
        @pl.when(r == 0)
        def _():
            st_s[...] = jnp.zeros(st_s.shape, F32)

    def block(c, carry):
        r0 = pl.multiple_of(c * big, big)
        qb = q_s[pl.ds(r0, big), :]
        kb = k_s[pl.ds(r0, big), :]
        gb = g_s[pl.ds(r0, big), :]
        hi = gb.astype(BF16)
        lo = (gb - hi.astype(F32)).astype(BF16)
        seg2 = _dot(seg_ref[...], jnp.concatenate([hi, lo], axis=1))
        seg = seg2[:, :kw] + seg2[:, kw:]
        for h in range(GLA_HEADS):
            ks = slice(h * GLA_K_DIM, (h + 1) * GLA_K_DIM)
            q_h, k_h, seg_h = qb[:, ks], kb[:, ks], seg[:, ks]
            v_h = v_s[pl.ds(r0, big), h * GLA_V_DIM:(h + 1) * GLA_V_DIM]
            att = _gla_intra(q_h, k_h, seg_h, up_ref, same_ref, levels)
            o_h = _dot(att.astype(BF16), v_h)
            cb = seg_h[0:big, :]
            qe = q_h * jnp.exp(cb)
            if has_state:
                n_runs = big // tr
                v_hf = v_h.astype(F32)
                pad_k = jnp.zeros((2 * SUBLANES - tr, GLA_K_DIM), F32)
                pad_v = jnp.zeros((2 * SUBLANES - tr, GLA_V_DIM), F32)
                inter = []
                for s in range(n_runs):
                    rs = slice(s * tr, (s + 1) * tr)
                    seq = c * n_runs + s
                    st = s0_ref[seq, h].T
                    qep = jnp.concatenate([qe[rs], pad_k], axis=0).astype(BF16)
                    inter.append(_dot_nt(qep, st.astype(BF16))[0:tr])
                    cb_last = cb[(s + 1) * tr - 1:(s + 1) * tr, :]
                    kt = k_h[rs] * jnp.exp(cb_last - cb[rs])
                    ktp = jnp.concatenate([kt, pad_k], axis=0).astype(BF16)
                    vtp = jnp.concatenate([v_hf[rs], pad_v], axis=0).astype(BF16)
                    st_new = st * jnp.exp(cb_last) + _dot_tn(vtp, ktp)
                    sout_ref[seq, h] = st_new.T
                o_h = o_h + jnp.concatenate(inter, axis=0)
            else:
                st = st_s[h]
                o_h = o_h + _dot_nt(qe.astype(BF16), st.astype(BF16))
                cb_last = cb[big - 1:big, :]
                kt = (k_h * jnp.exp(cb_last - cb)).astype(BF16)
                st_s[h] = st * jnp.exp(cb_last) + _dot_tn(v_h, kt)
            o_s[pl.ds(r0, big), h * GLA_V_DIM:(h + 1) * GLA_V_DIM] = o_h
        return carry

    lax.fori_loop(0, rows // big, block, 0)

    if not has_state:
        @pl.when(r == pl.num_programs(1) - 1)
        def _():
            for h in range(GLA_HEADS):
                sout_ref[0, h] = st_s[h].T

    nw = nw_ref[...]
    for h in range(GLA_HEADS):
        vs = slice(h * GLA_V_DIM, (h + 1) * GLA_V_DIM)
        o_h = o_s[:, vs]
        ms = jnp.mean(o_h * o_h, axis=-1, keepdims=True)
        gate = _dot(xb, w_ref[:, 2 * kw + vw + h * GLA_V_DIM:2 * kw + vw + (h + 1) * GLA_V_DIM])
        o_s[:, vs] = o_h * lax.rsqrt(ms + RMS_EPS) * nw * _silu(gate)
    acc = _dot(o_s[...].astype(BF16), wo_ref[...])
    y_ref[...] = _layer_norm_rows(DN_ALPHA * x + acc, lng_ref[...], lnb_ref[...])


def _odd_layer(x2d, n_seq, t_seq, tn, tr, w_in, wga, wgb, bg, nw, wo, lng, lnb, state0, layer_j):
    has_state = state0 is not None
    seg, up, same, levels = _gla_constants(tr if has_state else GLA_BLOCK)
    n_row_tiles = t_seq // tr
    rows = tn * tr
    total = n_seq * t_seq
    kern = functools.partial(_odd_kernel, tn=tn, tr=tr, levels=levels, has_state=has_state)
    row_map = lambda n, r: (n * n_row_tiles + r, 0)
    c2 = lambda n, r: (0, 0)
    c3 = lambda n, r: (0, 0, 0)
    st_map = lambda n, r: (n, 0, 0, 0)
    in_specs = [
        pl.BlockSpec((rows, D_MODEL), row_map),
        pl.BlockSpec(w_in.shape, c2), pl.BlockSpec(wga.shape, c2), pl.BlockSpec(wgb.shape, c2),
        pl.BlockSpec(bg.shape, c2), pl.BlockSpec(nw.shape, c2), pl.BlockSpec(wo.shape, c2),
        pl.BlockSpec(lng.shape, c2), pl.BlockSpec(lnb.shape, c2),
        pl.BlockSpec(seg.shape, c2), pl.BlockSpec(up.shape, c3), pl.BlockSpec(same.shape, c3),
    ]
    args = [x2d, w_in, wga, wgb, bg, nw, wo, lng, lnb, seg, up, same]
    st_block = (tn, GLA_HEADS, GLA_K_DIM, GLA_V_DIM)
    if has_state:
        in_specs.append(pl.BlockSpec((tn, None, GLA_HEADS, GLA_K_DIM, GLA_V_DIM),
                                     lambda n, r: (n, layer_j, 0, 0, 0)))
        args.append(state0)
    kw = GLA_HEADS * GLA_K_DIM
    vw = GLA_HEADS * GLA_V_DIM
    return pl.pallas_call(
        kern,
        grid=(n_seq // tn, n_row_tiles),
        in_specs=in_specs,
        out_specs=[pl.BlockSpec((rows, D_MODEL), row_map), pl.BlockSpec(st_block, st_map)],
        out_shape=[jax.ShapeDtypeStruct((total, D_MODEL), F32),
                   jax.ShapeDtypeStruct((n_seq, GLA_HEADS, GLA_K_DIM, GLA_V_DIM), F32)],
        scratch_shapes=[pltpu.VMEM((rows, kw), F32), pltpu.VMEM((rows, kw), F32), pltpu.VMEM((rows, vw), BF16),
                        pltpu.VMEM((rows, kw), F32), pltpu.VMEM((rows, vw), F32),
                        pltpu.VMEM((GLA_HEADS, GLA_V_DIM, GLA_K_DIM), F32)],
        compiler_params=_cparams(("arbitrary", "arbitrary")),
        name="odd_layer",
    )(*args)


def _rope_tables(positions):
    half = ROT_DIM // 2
    inv_freq = jnp.power(jnp.float32(ROPE_THETA), -jnp.arange(0, ROT_DIM, 2, dtype=jnp.float32) / ROT_DIM)
    ang = positions.astype(jnp.float32)[:, None] * inv_freq[None, :]
    cos, sin = jnp.cos(ang), jnp.sin(ang)
    n = positions.shape[0]
    ones = jnp.ones((n, DA_QK_DIM - ROT_DIM), F32)
    zeros = jnp.zeros((n, DA_QK_DIM - ROT_DIM), F32)
    zh = jnp.zeros((n, half), F32)
    cos_map = jnp.concatenate([cos, cos, ones], axis=1)
    sina_map = jnp.concatenate([zh, sin, zeros], axis=1)
    sinb_map = jnp.concatenate([-sin, zh, zeros], axis=1)
    two = lambda t: jnp.concatenate([t, t], axis=1)
    return two(cos_map), two(sina_map), two(sinb_map)


def kernel(x_prompt, x_sample, cache_k, cache_v, state_pool, state_gla, page_table, w_in_even, w_pool_lin,
           pool_scale, diff_lambda_params, subln_w, w_out_even, w_in_odd, w_gate_a, w_gate_b, b_gate, gla_norm_w,
           w_out_odd, ln_g, ln_b):
    n_p, t_p, _ = x_prompt.shape
    n_s, t_s, _ = x_sample.shape
    n_pool, n_even, page_size = cache_k.shape[:3]
    n_pages = page_table.shape[1]
    past_len = n_pages * page_size

    tr_p = min(512, t_p)
    tq_p = min(512, t_p)
    tn_even_s = 16
    tn_odd_s = GLA_BLOCK // t_s

    w_in_even_b = w_in_even.astype(BF16)
    w_pool_b = w_pool_lin.astype(BF16)
    w_out_even_b = w_out_even.astype(BF16)
    w_in_odd_b = w_in_odd.astype(BF16)
    w_out_odd_b = w_out_odd.astype(BF16)
    pad_rank = LANES - GLA_GATE_RANK
    w_ga_b = jnp.pad(w_gate_a, ((0, 0), (0, 0), (0, pad_rank))).astype(BF16)
    w_gb_b = jnp.pad(w_gate_b, ((0, 0), (0, pad_rank), (0, 0))).astype(BF16)

    cos_p, sina_p, sinb_p = _rope_tables(jnp.arange(t_p))
    cos_s, sina_s, sinb_s = (jnp.tile(t, (tn_even_s, 1)) for t in _rope_tables(past_len + jnp.arange(t_s)))

    ck = cache_k.reshape(n_pool, n_even, page_size * DA_HEADS, 2 * DA_QK_DIM)
    cv = cache_v.reshape(n_pool, n_even, page_size * DA_HEADS, DA_V_DIM)
    pt_flat = page_table.reshape(-1).astype(jnp.int32)
    pool16 = jnp.pad(state_pool, ((0, 0), (0, 0), (POOL_HIST - state_pool.shape[2], 0), (0, 0)))

    yp = x_prompt.reshape(n_p * t_p, D_MODEL)
    ys = x_sample.reshape(n_s * t_s, D_MODEL)
    kp, vp, pp, gp, ksm, vsm, psm, gsm = [], [], [], [], [], [], [], []
    for i in range(DEPTH):
        j = i // 2
        lng, lnb = ln_g[i][None, :], ln_b[i][None, :]
        if i % 2 == 0:
            lam_init = 0.8 - 0.6 * math.exp(-0.3 * i)
            pscale = pool_scale[j][None, :]
            sub_w = subln_w[j][None, :]
            lam = diff_lambda_params[j]
            w_vt = w_in_even_b[j][:, 2 * POOL_WIDTH + 2 * DA_WIDTH:2 * POOL_WIDTH + 3 * DA_WIDTH].T
            pmix, q, k32, v32, ga, npool, kbf, vt = _even_in(
                yp, n_p, t_p, 1, tr_p, tq_p, 0, w_in_even_b[j], w_pool_b[j], pscale, cos_p, sina_p, sinb_p, w_vt,
                None, j, BF16)
            omix = _attn_prompt(q, kbf, vt, ga, lam, sub_w, n_p, t_p, tq_p, lam_init)
            yp = _out_ln(pmix, omix, yp, w_out_even_b[j], lng, lnb, tr_p)
            kp.append(k32.reshape(n_p, t_p, DA_HEADS, 2 * DA_QK_DIM))
            vp.append(v32.reshape(n_p, t_p, DA_HEADS, DA_V_DIM))
            pp.append(npool[:, 1:])
            pmix, q, k32, v32, ga, npool = _even_in(
                ys, n_s, t_s, tn_even_s, t_s, t_s, past_len, w_in_even_b[j], w_pool_b[j], pscale, cos_s, sina_s,
                sinb_s, None, pool16, j, F32)
            omix = _attn_sample(pt_flat, q, ck, cv, k32, v32, ga, lam, sub_w, j, n_s, t_s, n_pages, lam_init)
            ys = _out_ln(pmix, omix, ys, w_out_even_b[j], lng, lnb, min(512, n_s * t_s))
            ksm.append(k32.reshape(n_s, t_s, DA_HEADS, 2 * DA_QK_DIM))
            vsm.append(v32.reshape(n_s, t_s, DA_HEADS, DA_V_DIM))
            psm.append(npool[:, 1:])
        else:
            ow = (w_in_odd_b[j], w_ga_b[j], w_gb_b[j], b_gate[j][None, :], gla_norm_w[j][None, :], w_out_odd_b[j],
                  lng, lnb)
            yp, s_p = _odd_layer(yp, n_p, t_p, 1, tr_p, *ow, None, j)
            ys, s_s = _odd_layer(ys, n_s, t_s, tn_odd_s, t_s, *ow, state_gla, j)
            gp.append(s_p)
            gsm.append(s_s)
    return (yp.reshape(n_p, t_p, D_MODEL), ys.reshape(n_s, t_s, D_MODEL),
            jnp.stack(kp, axis=1), jnp.stack(vp, axis=1), jnp.stack(pp, axis=1), jnp.stack(gp, axis=1),
            jnp.stack(ksm, axis=1), jnp.stack(vsm, axis=1), jnp.stack(psm, axis=1), jnp.stack(gsm, axis=1))
```

```python
import functools
import math

import numpy as np
import jax
import jax.numpy as jnp
from jax import lax
from jax.experimental import pallas as pl
from jax.experimental.pallas import tpu as pltpu

F32 = jnp.float32
BF16 = jnp.bfloat16

LANES = 128
SUBLANES = 8
VMEM_LIMIT_BYTES = 56 * 1024 * 1024

D_MODEL = 1024
DEPTH = 4
POOL_WIDTH = 512
POOL_WINDOWS = (2, 4, 8, 16)
POOL_GROUP_DIM = 128
POOL_HIST = 16
DA_HEADS = 4
DA_QK_DIM = 64
DA_V_DIM = 128
DA_WIDTH = 512
ROT_DIM = 16
ROPE_THETA = 500000.0
GLA_HEADS = 4
GLA_K_DIM = 128
GLA_V_DIM = 256
GLA_GATE_RANK = 16
GLA_TAU = 16.0
GLA_BLOCK = 64
DN_ALPHA = (2 * DEPTH) ** 0.25
LN_EPS = 1e-5
RMS_EPS = 1e-5

_NT = (((1,), (1,)), ((), ()))
_TN = (((0,), (0,)), ((), ()))


def _dot(a, b):
    return jnp.dot(a, b, preferred_element_type=F32)


def _dot_nt(a, b):
    return lax.dot_general(a, b, _NT, preferred_element_type=F32)


def _dot_tn(a, b):
    return lax.dot_general(a, b, _TN, preferred_element_type=F32)


def _silu(x):
    return x * jax.nn.sigmoid(x)


def _cparams(sem):
    return pltpu.CompilerParams(dimension_semantics=sem, vmem_limit_bytes=VMEM_LIMIT_BYTES)


def _layer_norm_rows(z, g, b):
    mu = jnp.mean(z, axis=-1, keepdims=True)
    zc = z - mu
    var = jnp.mean(zc * zc, axis=-1, keepdims=True)
    return zc * lax.rsqrt(var + LN_EPS) * g + b


def _store_head_rows(out_ref, val, h, tn, tr):
    for s in range(tn):
        out_ref[s, pl.ds(h, tr, stride=DA_HEADS), :] = val[s * tr:(s + 1) * tr, :]


def _even_in_kernel(*refs, tn, tr, tk, pos0, has_past, n_alias):
    refs = refs[:7 + 1] + refs[7 + 1 + n_alias:]
    if has_past:
        (x_ref, w_ref, wlin_ref, pscale_ref, cos_ref, sina_ref, sinb_ref, buf_ref,
         pmix_ref, q_ref, k32_ref, v32_ref, ga_ref, npool_ref, ext_ref) = refs
        wvt_ref = kbf_ref = vt_ref = None
    else:
        (x_ref, w_ref, wlin_ref, pscale_ref, cos_ref, sina_ref, sinb_ref, wvt_ref,
         pmix_ref, q_ref, k32_ref, v32_ref, ga_ref, npool_ref, kbf_ref, vt_ref, ext_ref) = refs
        buf_ref = None
    r = pl.program_id(1)
    rows = tn * tr
    xb = x_ref[...].astype(BF16)

    def proj(c0, n):
        return _dot(xb, w_ref[:, c0:c0 + n])

    pu = proj(0, POOL_WIDTH)
    pg = proj(POOL_WIDTH, POOL_WIDTH)
    if has_past:
        ext_ref[:, 0:POOL_HIST, :] = buf_ref[...]
    else:
        @pl.when(r == 0)
        def _():
            ext_ref[:, 0:POOL_HIST, :] = jnp.zeros((tn, POOL_HIST, POOL_WIDTH), F32)

        @pl.when(r > 0)
        def _():
            ext_ref[:, 0:POOL_HIST, :] = ext_ref[:, tr:tr + POOL_HIST, :]
    for s in range(tn):
        ext_ref[s, POOL_HIST:POOL_HIST + tr, :] = pu[s * tr:(s + 1) * tr, :]
    npool_ref[...] = ext_ref[:, tr:tr + POOL_HIST, :]

    pos = pos0 + r * tr + lax.broadcasted_iota(jnp.int32, (tr, POOL_GROUP_DIM), 0)
    for g, w in enumerate(POOL_WINDOWS):
        c0 = g * POOL_GROUP_DIM
        cnt = jnp.minimum(pos + 1, w).astype(F32)
        pieces = []
        for s in range(tn):
            u_g = ext_ref[s, POOL_HIST:POOL_HIST + tr, c0:c0 + POOL_GROUP_DIM]
            acc = u_g
            for i in range(1, w):
                acc = acc + ext_ref[s, POOL_HIST - i:POOL_HIST - i + tr, c0:c0 + POOL_GROUP_DIM]
            pieces.append(acc / cnt - u_g)
        d = pieces[0] if tn == 1 else jnp.concatenate(pieces, axis=0)
        y = _dot(d.astype(BF16), wlin_ref[g]) * pscale_ref[:, c0:c0 + POOL_GROUP_DIM]
        pmix_ref[:, c0:c0 + POOL_GROUP_DIM] = (y * _silu(pg[:, c0:c0 + POOL_GROUP_DIM])).astype(pmix_ref.dtype)

    cos = cos_ref[...]
    sina = sina_ref[...]
    sinb = sinb_ref[...]

    def rope(xh):
        return (xh * cos + pltpu.roll(xh, ROT_DIM // 2, 1) * sina
                + pltpu.roll(xh, LANES - ROT_DIM // 2, 1) * sinb)

    q = proj(2 * POOL_WIDTH, DA_WIDTH)
    k = proj(2 * POOL_WIDTH + DA_WIDTH, DA_WIDTH)
    for h in range(DA_HEADS):
        sl = slice(h * LANES, (h + 1) * LANES)
        q_scale = DA_QK_DIM ** -0.5 * (1.0 if has_past else math.log2(math.e))
        q_ref[:, sl] = (rope(q[:, sl]) * q_scale).astype(q_ref.dtype)
        kr = rope(k[:, sl])
        _store_head_rows(k32_ref, kr, h, tn, tr)
        if kbf_ref is not None:
            kbf_ref[:, sl] = kr.astype(kbf_ref.dtype)
    v = proj(2 * POOL_WIDTH + 2 * DA_WIDTH, DA_WIDTH)
    for h in range(DA_HEADS):
        _store_head_rows(v32_ref, v[:, h * LANES:(h + 1) * LANES], h, tn, tr)
    if vt_ref is not None:
        vt = _dot_nt(wvt_ref[...], xb).astype(vt_ref.dtype)
        for h in range(DA_HEADS):
            for c in range(tr // tk):
                vt_ref[h, c] = vt[h * DA_V_DIM:(h + 1) * DA_V_DIM, c * tk:(c + 1) * tk]
    ga_ref[...] = _silu(proj(2 * POOL_WIDTH + 3 * DA_WIDTH, DA_WIDTH)).astype(ga_ref.dtype)


def _even_in(x2d, n_seq, t_seq, tn, tr, tk, pos0, w_in, wlin, pscale, cos, sina, sinb, w_vt, buf16, layer_j,
             n_even, kv_prev, act_dtype):
    n_row_tiles = t_seq // tr
    n_seq_tiles = n_seq // tn
    rows = tn * tr
    total = n_seq * t_seq
    has_past = buf16 is not None
    n_alias = 0 if kv_prev is None else 2
    kern = functools.partial(_even_in_kernel, tn=tn, tr=tr, tk=tk, pos0=pos0, has_past=has_past, n_alias=n_alias)
    row_map = lambda n, r: (n * n_row_tiles + r, 0)
    const2 = lambda n, r: (0, 0)
    tab_map = (lambda n, r: (r, 0)) if not has_past else const2
    in_specs = [
        pl.BlockSpec((rows, D_MODEL), row_map),
        pl.BlockSpec(w_in.shape, const2),
        pl.BlockSpec(wlin.shape, lambda n, r: (0, 0, 0)),
        pl.BlockSpec(pscale.shape, const2),
        pl.BlockSpec((rows, LANES), tab_map),
        pl.BlockSpec((rows, LANES), tab_map),
        pl.BlockSpec((rows, LANES), tab_map),
    ]
    args = [x2d, w_in, wlin, pscale, cos, sina, sinb]
    blk = lambda: pl.BlockSpec((rows, DA_WIDTH), row_map)
    sds = lambda dt: jax.ShapeDtypeStruct((total, DA_WIDTH), dt)
    kv_blk = lambda: pl.BlockSpec((tn, None, tr * DA_HEADS, LANES), lambda n, r: (n, layer_j, r, 0))
    kv_sds = jax.ShapeDtypeStruct((n_seq, n_even, t_seq * DA_HEADS, LANES), F32)
    out_specs = [blk(), blk(), kv_blk(), kv_blk(), blk(),
                 pl.BlockSpec((tn, POOL_HIST, POOL_WIDTH), lambda n, r: (n, 0, 0))]
    out_shape = [sds(act_dtype), sds(act_dtype), kv_sds, kv_sds, sds(act_dtype),
                 jax.ShapeDtypeStruct((n_seq, POOL_HIST, POOL_WIDTH), F32)]
    if has_past:
        in_specs.append(pl.BlockSpec((tn, None, POOL_HIST, POOL_WIDTH), lambda n, r: (n, layer_j, 0, 0)))
        args.append(buf16)
    else:
        in_specs.append(pl.BlockSpec(w_vt.shape, const2))
        args.append(w_vt)
        out_specs += [blk(), pl.BlockSpec((None, DA_HEADS, tr // tk, DA_V_DIM, tk), lambda n, r: (n, 0, r, 0, 0))]
        out_shape += [sds(BF16), jax.ShapeDtypeStruct((n_seq, DA_HEADS, t_seq // tk, DA_V_DIM, tk), BF16)]
    aliases = {}
    if kv_prev is not None:
        aliases = {len(args): 2, len(args) + 1: 3}
        in_specs += [pl.BlockSpec(memory_space=pl.ANY), pl.BlockSpec(memory_space=pl.ANY)]
        args += list(kv_prev)
    return pl.pallas_call(
        kern,
        grid=(n_seq_tiles, n_row_tiles),
        in_specs=in_specs,
        out_specs=out_specs,
        out_shape=out_shape,
        input_output_aliases=aliases,
        scratch_shapes=[pltpu.VMEM((tn, POOL_HIST + tr, POOL_WIDTH), F32)],
        compiler_params=_cparams(("arbitrary", "arbitrary")),
        name="even_in",
    )(*args)


def _lambda_full(lam_ref, lam_init):
    lam = lam_ref[...]
    a = jnp.sum(lam[0:1, :] * lam[1:2, :], axis=1, keepdims=True)
    b = jnp.sum(lam[2:3, :] * lam[3:4, :], axis=1, keepdims=True)
    return jnp.exp(a) - jnp.exp(b) + lam_init


def _diff_finish(o1, o2, lam_full, lam_init, sub_w, gate):
    o = o1 - lam_full * o2
    ms = jnp.mean(o * o, axis=-1, keepdims=True)
    o = o * lax.rsqrt(ms + RMS_EPS) * sub_w * (1.0 - lam_init)
    return o * gate


def _stack_maps(q):
    lane = lax.broadcasted_iota(jnp.int32, q.shape, 1)
    zero = jnp.zeros_like(q)
    return jnp.concatenate([jnp.where(lane < DA_QK_DIM, q, zero), jnp.where(lane >= DA_QK_DIM, q, zero)], axis=0)


def _attn_prompt_kernel(q_ref, k_ref, vt_ref, ga_ref, lam_ref, sub_ref, o_ref, acc_ref, sa_ref, sb_ref,
                        *, tq, lam_init):
    qi = pl.program_id(2)
    qs = _stack_maps(q_ref[...])
    acc_ref[...] = jnp.zeros(acc_ref.shape, F32)

    def scores(ki, s_ref):
        start = pl.multiple_of(ki * tq, tq)
        s = _dot_nt(k_ref[pl.ds(start, tq), :], qs)
        s_ref[...] = s
        return jnp.max(s, axis=0, keepdims=True)

    def mask_diagonal(s_ref):
        s = s_ref[...]
        key = lax.broadcasted_iota(jnp.int32, s.shape, 0)
        qry = lax.broadcasted_iota(jnp.int32, s.shape, 1)
        qry = jnp.where(qry >= tq, qry - tq, qry)
        s = jnp.where(key <= qry, s, -jnp.inf)
        s_ref[...] = s
        return jnp.max(s, axis=0, keepdims=True)

    def absorb(ki, s_ref, mx, m_prev, l_prev):
        m_new = jnp.maximum(m_prev, mx)
        alpha = jnp.exp2(m_prev - m_new)
        p = jnp.exp2(s_ref[...] - m_new)
        l_new = alpha * l_prev + jnp.sum(p, axis=0, keepdims=True)
        acc_ref[...] = alpha * acc_ref[...] + _dot(vt_ref[ki], p.astype(BF16))
        return m_new, l_new

    n_pairs = qi // 2
    mx0 = scores(0, sa_ref)

    def pair(i, c):
        m, l, mxa = c
        mxb = scores(2 * i + 1, sb_ref)
        m, l = absorb(2 * i, sa_ref, mxa, m, l)
        mxa = scores(2 * i + 2, sa_ref)
        m, l = absorb(2 * i + 1, sb_ref, mxb, m, l)
        return m, l, mxa

    m0 = jnp.full((1, 2 * tq), -jnp.inf, F32)
    l0 = jnp.zeros((1, 2 * tq), F32)
    m, l, mxa = lax.fori_loop(0, n_pairs, pair, (m0, l0, mx0))

    def tail_even(m, l, mxa):
        del mxa
        return absorb(qi, sa_ref, mask_diagonal(sa_ref), m, l)

    def tail_odd(m, l, mxa):
        scores(qi, sb_ref)
        m, l = absorb(qi - 1, sa_ref, mxa, m, l)
        return absorb(qi, sb_ref, mask_diagonal(sb_ref), m, l)

    m, l = lax.cond(qi % 2 == 0, tail_even, tail_odd, m, l, mxa)

    o_t = acc_ref[...] / l
    o1 = o_t[:, :tq].T
    o2 = o_t[:, tq:].T
    out = _diff_finish(o1, o2, _lambda_full(lam_ref, lam_init), lam_init, sub_ref[...], ga_ref[...].astype(F32))
    o_ref[...] = out.astype(o_ref.dtype)


def _attn_prompt(q, kbf, vt, ga, lam, sub_w, n_seq, t_seq, tq, lam_init):
    nq = t_seq // tq
    kern = functools.partial(_attn_prompt_kernel, tq=tq, lam_init=lam_init)
    qmap = lambda b, h, i: (b * nq + i, h)
    c2 = lambda b, h, i: (0, 0)
    return pl.pallas_call(
        kern,
        grid=(n_seq, DA_HEADS, nq),
        in_specs=[
            pl.BlockSpec((tq, LANES), qmap),
            pl.BlockSpec((t_seq, LANES), lambda b, h, i: (b, h)),
            pl.BlockSpec((None, None, nq, DA_V_DIM, tq), lambda b, h, i: (b, h, 0, 0, 0)),
            pl.BlockSpec((tq, LANES), qmap),
            pl.BlockSpec(lam.shape, c2),
            pl.BlockSpec(sub_w.shape, c2),
        ],
        out_specs=pl.BlockSpec((tq, LANES), qmap),
        out_shape=jax.ShapeDtypeStruct((n_seq * t_seq, DA_WIDTH), BF16),
        scratch_shapes=[pltpu.VMEM((DA_V_DIM, 2 * tq), F32), pltpu.VMEM((tq, 2 * tq), F32),
                        pltpu.VMEM((tq, 2 * tq), F32)],
        compiler_params=_cparams(("arbitrary", "arbitrary", "arbitrary")),
        name="attn_prompt",
    )(q, kbf, vt, ga, lam, sub_w)


def _attn_sample_kernel(*refs, t_new, n_pages, lam_init):
    pt_ref, q_ref = refs[0], refs[1]
    k_refs = refs[2:2 + n_pages]
    v_refs = refs[2 + n_pages:2 + 2 * n_pages]
    (kn_ref, vn_ref, ga_ref, lam_ref, sub_ref, bias_ref, biasn_ref, o_ref, s_ref) = refs[2 + 2 * n_pages:]
    del pt_ref
    n_rows = 2 * DA_HEADS * t_new

    pieces = [None] * (2 * DA_HEADS)
    for h in range(DA_HEADS):
        st = _stack_maps(q_ref[:, h * LANES:(h + 1) * LANES])
        pieces[h] = st[:t_new]
        pieces[DA_HEADS + h] = st[t_new:]
    wq = jnp.concatenate(pieces, axis=0).astype(BF16)

    bias = bias_ref[...]
    mx = None
    for p in range(n_pages):
        s = _dot_nt(wq, k_refs[p][...].astype(BF16)) + bias
        s_ref[p] = s
        mx = s if mx is None else jnp.maximum(mx, s)
    kn = kn_ref[...]
    vn = vn_ref[...]
    sn = _dot_nt(wq, kn.astype(BF16)) + biasn_ref[...]
    m = jnp.maximum(jnp.max(mx, axis=1, keepdims=True), jnp.max(sn, axis=1, keepdims=True))

    pn = jnp.exp(sn - m)
    acc = _dot(pn.astype(BF16), vn.astype(BF16))
    lsum = None
    for p in range(n_pages):
        pe = jnp.exp(s_ref[p] - m)
        lsum = pe if lsum is None else lsum + pe
        acc = acc + _dot(pe.astype(BF16), v_refs[p][...].astype(BF16))
    l = jnp.sum(lsum, axis=1, keepdims=True) + jnp.sum(pn, axis=1, keepdims=True)

    half = n_rows // 2
    o1 = acc[:half] / l[:half]
    o2 = acc[half:] / l[half:]
    gate = jnp.concatenate([ga_ref[:, h * LANES:(h + 1) * LANES] for h in range(DA_HEADS)], axis=0)
    out = _diff_finish(o1, o2, _lambda_full(lam_ref, lam_init), lam_init, sub_ref[...], gate)
    for h in range(DA_HEADS):
        o_ref[:, h * LANES:(h + 1) * LANES] = out[h * t_new:(h + 1) * t_new].astype(o_ref.dtype)


def _sample_biases(t_new, page_rows):
    n_rows = 2 * DA_HEADS * t_new
    row = np.arange(n_rows)[:, None]
    row_head, row_q = (row // t_new) % DA_HEADS, row % t_new
    col = np.arange(page_rows)[None, :]
    bias = np.where(row_head == col % DA_HEADS, 0.0, -np.inf).astype(np.float32)
    coln = np.arange(DA_HEADS * t_new)[None, :]
    biasn = np.where((row_head == coln % DA_HEADS) & (coln // DA_HEADS <= row_q), 0.0, -np.inf).astype(np.float32)
    return jnp.asarray(bias), jnp.asarray(biasn)


def _attn_sample(page_table_flat, q, ck, cv, k_new, v_new, ga, lam, sub_w, layer_j, n_seq, t_new, n_pages,
                 lam_init):
    page_rows = ck.shape[2]
    kern = functools.partial(_attn_sample_kernel, t_new=t_new, n_pages=n_pages, lam_init=lam_init)
    bias, biasn = _sample_biases(t_new, page_rows)
    seq_map = lambda n, pt: (n, 0)
    const2 = lambda n, pt: (0, 0)
    n_rows = 2 * DA_HEADS * t_new

    def page_spec(p):
        return pl.BlockSpec((None, None, page_rows, LANES), lambda n, pt: (pt[n * n_pages + p], layer_j, 0, 0))

    seq_spec = lambda: pl.BlockSpec((t_new, DA_WIDTH), seq_map)
    new_spec = lambda: pl.BlockSpec((None, None, t_new * DA_HEADS, LANES), lambda n, pt: (n, layer_j, 0, 0))
    grid_spec = pltpu.PrefetchScalarGridSpec(
        num_scalar_prefetch=1,
        grid=(n_seq,),
        in_specs=([seq_spec()] + [page_spec(p) for p in range(n_pages)] + [page_spec(p) for p in range(n_pages)]
                  + [new_spec(), new_spec(), seq_spec(), pl.BlockSpec(lam.shape, const2),
                     pl.BlockSpec(sub_w.shape, const2), pl.BlockSpec(bias.shape, const2),
                     pl.BlockSpec(biasn.shape, const2)]),
        out_specs=seq_spec(),
        scratch_shapes=[pltpu.VMEM((n_pages, n_rows, page_rows), F32)],
    )
    return pl.pallas_call(
        kern,
        grid_spec=grid_spec,
        out_shape=jax.ShapeDtypeStruct((n_seq * t_new, DA_WIDTH), F32),
        compiler_params=_cparams(("arbitrary",)),
        name="attn_sample",
    )(page_table_flat, q, *([ck] * n_pages), *([cv] * n_pages), k_new, v_new, ga, lam, sub_w, bias, biasn)


def _out_ln_kernel(a_ref, b_ref, x_ref, w_ref, g_ref, bb_ref, y_ref):
    half = a_ref.shape[1]
    acc = _dot(a_ref[...].astype(BF16), w_ref[0:half, :]) + _dot(b_ref[...].astype(BF16), w_ref[half:2 * half, :])
    y_ref[...] = _layer_norm_rows(DN_ALPHA * x_ref[...] + acc, g_ref[...], bb_ref[...])


def _out_ln(a, b, x2d, w_out, ln_g, ln_b, tm):
    total = x2d.shape[0]
    row = lambda i: (i, 0)
    const = lambda i: (0, 0)
    return pl.pallas_call(
        _out_ln_kernel,
        grid=(total // tm,),
        in_specs=[
            pl.BlockSpec((tm, a.shape[1]), row),
            pl.BlockSpec((tm, b.shape[1]), row),
            pl.BlockSpec((tm, D_MODEL), row),
            pl.BlockSpec(w_out.shape, const),
            pl.BlockSpec(ln_g.shape, const),
            pl.BlockSpec(ln_b.shape, const),
        ],
        out_specs=pl.BlockSpec((tm, D_MODEL), row),
        out_shape=jax.ShapeDtypeStruct((total, D_MODEL), F32),
        compiler_params=_cparams(("arbitrary",)),
        name="out_ln",
    )(a, b, x2d, w_out, ln_g, ln_b)


def _gla_constants(top):
    big = GLA_BLOCK
    idx = np.arange(big)
    levels = int(round(math.log2(top)))
    seg = [((idx[:, None] // top == idx[None, :] // top) & (idx[None, :] <= idx[:, None])).astype(np.float32)]
    ups, sames = [], []
    n = top
    for _ in range(levels):
        start = (idx // n) * n
        rho = start + n // 2 - 1
        upper = idx > rho
        t = idx[None, :]
        c = np.where(upper[:, None], (t > rho[:, None]) & (t <= idx[:, None]), (t > idx[:, None]) & (t <= rho[:, None]))
        seg.append(c.astype(np.float32))
        ups.append(np.repeat(upper.astype(np.float32)[:, None], LANES, axis=1))
        sames.append((start[:, None] == start[None, :]).astype(np.float32))
        n //= 2
    return (jnp.asarray(np.concatenate(seg, axis=0), BF16), jnp.asarray(np.stack(ups)), jnp.asarray(np.stack(sames)),
            levels)


def _gla_intra(q_h, k_h, seg_h, up_ref, same_ref, levels):
    big = GLA_BLOCK
    att = jnp.zeros((big, big), F32)
    for lv in range(levels):
        e = jnp.exp(seg_h[(1 + lv) * big:(2 + lv) * big, :])
        up = up_ref[lv]
        qh = (q_h * e * up).astype(BF16)
        kh = (k_h * e * (1.0 - up)).astype(BF16)
        att = att + _dot_nt(qh, kh) * same_ref[lv]
    diag = jnp.sum(q_h * k_h, axis=1, keepdims=True)
    row = lax.broadcasted_iota(jnp.int32, (big, big), 0)
    col = lax.broadcasted_iota(jnp.int32, (big, big), 1)
    return att + jnp.where(row == col, diag, 0.0)


def _odd_kernel(*refs, tn, tr, levels, has_state, n_alias):
    n_in = 12 + (1 if has_state else 0)
    refs = refs[:n_in] + refs[n_in + n_alias:]
    if has_state:
        (x_ref, w_ref, wga_ref, wgb_ref, bg_ref, nw_ref, wo_ref, lng_ref, lnb_ref, seg_ref, up_ref, same_ref,
         s0_ref, y_ref, sout_ref, q_s, k_s, v_s, g_s, o_s, st_s) = refs
    else:
        (x_ref, w_ref, wga_ref, wgb_ref, bg_ref, nw_ref, wo_ref, lng_ref, lnb_ref, seg_ref, up_ref, same_ref,
         y_ref, sout_ref, q_s, k_s, v_s, g_s, o_s, st_s) = refs
        s0_ref = None
    r = pl.program_id(1)
    rows = tn * tr
    big = GLA_BLOCK
    kw = GLA_HEADS * GLA_K_DIM
    vw = GLA_HEADS * GLA_V_DIM
    x = x_ref[...]
    xb = x.astype(BF16)
    q_s[...] = _dot(xb, w_ref[:, 0:kw]) * (GLA_K_DIM ** -0.5)
    k_s[...] = _dot(xb, w_ref[:, kw:2 * kw])
    v_s[...] = _dot(xb, w_ref[:, 2 * kw:2 * kw + vw]).astype(BF16)
    low = _dot(xb, wga_ref[...])
    z = _dot(low.astype(BF16), wgb_ref[...]) + bg_ref[...]
    g_s[...] = (jnp.minimum(z, 0.0) - jnp.log1p(jnp.exp(-jnp.abs(z)))) / GLA_TAU

    if not has_state:
        @pl.when(r == 0)
        def _():
            st_s[...] = jnp.zeros(st_s.shape, F32)

    def block(c, carry):
        r0 = pl.multiple_of(c * big, big)
        qb = q_s[pl.ds(r0, big), :]
        kb = k_s[pl.ds(r0, big), :]
        gb = g_s[pl.ds(r0, big), :]
        hi = gb.astype(BF16)
        lo = (gb - hi.astype(F32)).astype(BF16)
        seg2 = _dot(seg_ref[...], jnp.concatenate([hi, lo], axis=1))
        seg = seg2[:, :kw] + seg2[:, kw:]
        for h in range(GLA_HEADS):
            ks = slice(h * GLA_K_DIM, (h + 1) * GLA_K_DIM)
            q_h, k_h, seg_h = qb[:, ks], kb[:, ks], seg[:, ks]
            v_h = v_s[pl.ds(r0, big), h * GLA_V_DIM:(h + 1) * GLA_V_DIM]
            att = _gla_intra(q_h, k_h, seg_h, up_ref, same_ref, levels)
            o_h = _dot(att.astype(BF16), v_h)
            cb = seg_h[0:big, :]
            qe = q_h * jnp.exp(cb)
            if has_state:
                n_runs = big // tr
                v_hf = v_h.astype(F32)
                pad_k = jnp.zeros((2 * SUBLANES - tr, GLA_K_DIM), F32)
                pad_v = jnp.zeros((2 * SUBLANES - tr, GLA_V_DIM), F32)
                inter = []
                for s in range(n_runs):
                    rs = slice(s * tr, (s + 1) * tr)
                    seq = c * n_runs + s
                    st = s0_ref[seq, h].T
                    qep = jnp.concatenate([qe[rs], pad_k], axis=0).astype(BF16)
                    inter.append(_dot_nt(qep, st.astype(BF16))[0:tr])
                    cb_last = cb[(s + 1) * tr - 1:(s + 1) * tr, :]
                    kt = k_h[rs] * jnp.exp(cb_last - cb[rs])
                    ktp = jnp.concatenate([kt, pad_k], axis=0).astype(BF16)
                    vtp = jnp.concatenate([v_hf[rs], pad_v], axis=0).astype(BF16)
                    st_new = st * jnp.exp(cb_last) + _dot_tn(vtp, ktp)
                    sout_ref[seq, h] = st_new.T
                o_h = o_h + jnp.concatenate(inter, axis=0)
            else:
                st = st_s[h]
                o_h = o_h + _dot_nt(qe.astype(BF16), st.astype(BF16))
                cb_last = cb[big - 1:big, :]
                kt = (k_h * jnp.exp(cb_last - cb)).astype(BF16)
                st_s[h] = st * jnp.exp(cb_last) + _dot_tn(v_h, kt)
            o_s[pl.ds(r0, big), h * GLA_V_DIM:(h + 1) * GLA_V_DIM] = o_h
        return carry

    lax.fori_loop(0, rows // big, block, 0, unroll=2 if rows // big >= 2 else 1)

    if not has_state:
        @pl.when(r == pl.num_programs(1) - 1)
        def _():
            for h in range(GLA_HEADS):
                sout_ref[0, h] = st_s[h].T

    nw = nw_ref[...]
    for h in range(GLA_HEADS):
        vs = slice(h * GLA_V_DIM, (h + 1) * GLA_V_DIM)
        o_h = o_s[:, vs]
        ms = jnp.mean(o_h * o_h, axis=-1, keepdims=True)
        gate = _dot(xb, w_ref[:, 2 * kw + vw + h * GLA_V_DIM:2 * kw + vw + (h + 1) * GLA_V_DIM])
        o_s[:, vs] = o_h * lax.rsqrt(ms + RMS_EPS) * nw * _silu(gate)
    acc = _dot(o_s[...].astype(BF16), wo_ref[...])
    y_ref[...] = _layer_norm_rows(DN_ALPHA * x + acc, lng_ref[...], lnb_ref[...])


def _odd_layer(x2d, n_seq, t_seq, tn, tr, w_in, wga, wgb, bg, nw, wo, lng, lnb, state0, layer_j, n_odd,
               state_prev):
    has_state = state0 is not None
    seg, up, same, levels = _gla_constants(tr if has_state else GLA_BLOCK)
    n_row_tiles = t_seq // tr
    rows = tn * tr
    total = n_seq * t_seq
    n_alias = 0 if state_prev is None else 1
    kern = functools.partial(_odd_kernel, tn=tn, tr=tr, levels=levels, has_state=has_state, n_alias=n_alias)
    row_map = lambda n, r: (n * n_row_tiles + r, 0)
    c2 = lambda n, r: (0, 0)
    c3 = lambda n, r: (0, 0, 0)
    st_map = lambda n, r: (n, layer_j, 0, 0, 0)
    in_specs = [
        pl.BlockSpec((rows, D_MODEL), row_map),
        pl.BlockSpec(w_in.shape, c2), pl.BlockSpec(wga.shape, c2), pl.BlockSpec(wgb.shape, c2),
        pl.BlockSpec(bg.shape, c2), pl.BlockSpec(nw.shape, c2), pl.BlockSpec(wo.shape, c2),
        pl.BlockSpec(lng.shape, c2), pl.BlockSpec(lnb.shape, c2),
        pl.BlockSpec(seg.shape, c2), pl.BlockSpec(up.shape, c3), pl.BlockSpec(same.shape, c3),
    ]
    args = [x2d, w_in, wga, wgb, bg, nw, wo, lng, lnb, seg, up, same]
    st_block = (tn, None, GLA_HEADS, GLA_K_DIM, GLA_V_DIM)
    if has_state:
        in_specs.append(pl.BlockSpec(st_block, st_map))
        args.append(state0)
    aliases = {}
    if state_prev is not None:
        aliases = {len(args): 1}
        in_specs.append(pl.BlockSpec(memory_space=pl.ANY))
        args.append(state_prev)
    kw = GLA_HEADS * GLA_K_DIM
    vw = GLA_HEADS * GLA_V_DIM
    return pl.pallas_call(
        kern,
        grid=(n_seq // tn, n_row_tiles),
        in_specs=in_specs,
        out_specs=[pl.BlockSpec((rows, D_MODEL), row_map), pl.BlockSpec(st_block, st_map)],
        out_shape=[jax.ShapeDtypeStruct((total, D_MODEL), F32),
                   jax.ShapeDtypeStruct((n_seq, n_odd, GLA_HEADS, GLA_K_DIM, GLA_V_DIM), F32)],
        input_output_aliases=aliases,
        scratch_shapes=[pltpu.VMEM((rows, kw), F32), pltpu.VMEM((rows, kw), F32), pltpu.VMEM((rows, vw), BF16),
                        pltpu.VMEM((rows, kw), F32), pltpu.VMEM((rows, vw), F32),
                        pltpu.VMEM((GLA_HEADS, GLA_V_DIM, GLA_K_DIM), F32)],
        compiler_params=_cparams(("arbitrary", "arbitrary")),
        name="odd_layer",
    )(*args)


def _rope_tables(positions):
    half = ROT_DIM // 2
    inv_freq = jnp.power(jnp.float32(ROPE_THETA), -jnp.arange(0, ROT_DIM, 2, dtype=jnp.float32) / ROT_DIM)
    ang = positions.astype(jnp.float32)[:, None] * inv_freq[None, :]
    cos, sin = jnp.cos(ang), jnp.sin(ang)
    n = positions.shape[0]
    ones = jnp.ones((n, DA_QK_DIM - ROT_DIM), F32)
    zeros = jnp.zeros((n, DA_QK_DIM - ROT_DIM), F32)
    zh = jnp.zeros((n, half), F32)
    cos_map = jnp.concatenate([cos, cos, ones], axis=1)
    sina_map = jnp.concatenate([zh, sin, zeros], axis=1)
    sinb_map = jnp.concatenate([-sin, zh, zeros], axis=1)
    two = lambda t: jnp.concatenate([t, t], axis=1)
    return two(cos_map), two(sina_map), two(sinb_map)


def kernel(x_prompt, x_sample, cache_k, cache_v, state_pool, state_gla, page_table, w_in_even, w_pool_lin,
           pool_scale, diff_lambda_params, subln_w, w_out_even, w_in_odd, w_gate_a, w_gate_b, b_gate, gla_norm_w,
           w_out_odd, ln_g, ln_b):
    n_p, t_p, _ = x_prompt.shape
    n_s, t_s, _ = x_sample.shape
    n_pool, n_even, page_size = cache_k.shape[:3]
    n_pages = page_table.shape[1]
    past_len = n_pages * page_size

    tr_p = min(512, t_p)
    tq_p = min(512, t_p)
    tn_even_s = 16
    tn_odd_s = GLA_BLOCK // t_s

    w_in_even_b = w_in_even.astype(BF16)
    w_pool_b = w_pool_lin.astype(BF16)
    w_out_even_b = w_out_even.astype(BF16)
    w_in_odd_b = w_in_odd.astype(BF16)
    w_out_odd_b = w_out_odd.astype(BF16)
    pad_rank = LANES - GLA_GATE_RANK
    w_ga_b = jnp.pad(w_gate_a, ((0, 0), (0, 0), (0, pad_rank))).astype(BF16)
    w_gb_b = jnp.pad(w_gate_b, ((0, 0), (0, pad_rank), (0, 0))).astype(BF16)

    cos_p, sina_p, sinb_p = _rope_tables(jnp.arange(t_p))
    cos_s, sina_s, sinb_s = (jnp.tile(t, (tn_even_s, 1)) for t in _rope_tables(past_len + jnp.arange(t_s)))

    ck = cache_k.reshape(n_pool, n_even, page_size * DA_HEADS, 2 * DA_QK_DIM)
    cv = cache_v.reshape(n_pool, n_even, page_size * DA_HEADS, DA_V_DIM)
    pt_flat = page_table.reshape(-1).astype(jnp.int32)
    pool16 = jnp.pad(state_pool, ((0, 0), (0, 0), (POOL_HIST - state_pool.shape[2], 0), (0, 0)))

    yp = x_prompt.reshape(n_p * t_p, D_MODEL)
    ys = x_sample.reshape(n_s * t_s, D_MODEL)
    n_odd = state_gla.shape[1]
    pp, psm = [], []
    kv_p = tuple(jnp.zeros((n_p, n_even, t_p * DA_HEADS, LANES), F32) for _ in range(2))
    kv_s = tuple(jnp.zeros((n_s, n_even, t_s * DA_HEADS, LANES), F32) for _ in range(2))
    gla_p = jnp.zeros((n_p, n_odd) + state_gla.shape[2:], F32)
    gla_s = jnp.zeros((n_s, n_odd) + state_gla.shape[2:], F32)
    for i in range(DEPTH):
        j = i // 2
        lng, lnb = ln_g[i][None, :], ln_b[i][None, :]
        if i % 2 == 0:
            lam_init = 0.8 - 0.6 * math.exp(-0.3 * i)
            pscale = pool_scale[j][None, :]
            sub_w = subln_w[j][None, :]
            lam = diff_lambda_params[j]
            w_vt = w_in_even_b[j][:, 2 * POOL_WIDTH + 2 * DA_WIDTH:2 * POOL_WIDTH + 3 * DA_WIDTH].T
            pmix, q, k32, v32, ga, npool, kbf, vt = _even_in(
                yp, n_p, t_p, 1, tr_p, tq_p, 0, w_in_even_b[j], w_pool_b[j], pscale, cos_p, sina_p, sinb_p, w_vt,
                None, j, n_even, kv_p, BF16)
            kv_p = (k32, v32)
            omix = _attn_prompt(q, kbf, vt, ga, lam, sub_w, n_p, t_p, tq_p, lam_init)
            yp = _out_ln(pmix, omix, yp, w_out_even_b[j], lng, lnb, tr_p)
            pp.append(npool[:, 1:])
            pmix, q, k32, v32, ga, npool = _even_in(
                ys, n_s, t_s, tn_even_s, t_s, t_s, past_len, w_in_even_b[j], w_pool_b[j], pscale, cos_s, sina_s,
                sinb_s, None, pool16, j, n_even, kv_s, F32)
            kv_s = (k32, v32)
            omix = _attn_sample(pt_flat, q, ck, cv, k32, v32, ga, lam, sub_w, j, n_s, t_s, n_pages, lam_init)
            ys = _out_ln(pmix, omix, ys, w_out_even_b[j], lng, lnb, min(512, n_s * t_s))
            psm.append(npool[:, 1:])
        else:
            ow = (w_in_odd_b[j], w_ga_b[j], w_gb_b[j], b_gate[j][None, :], gla_norm_w[j][None, :], w_out_odd_b[j],
                  lng, lnb)
            yp, gla_p = _odd_layer(yp, n_p, t_p, 1, tr_p, *ow, None, j, n_odd, gla_p)
            ys, gla_s = _odd_layer(ys, n_s, t_s, tn_odd_s, t_s, *ow, state_gla, j, n_odd, gla_s)
    kv5 = lambda a, n, t: a.reshape(n, n_even, t, DA_HEADS, LANES)
    return (yp.reshape(n_p, t_p, D_MODEL), ys.reshape(n_s, t_s, D_MODEL),
            kv5(kv_p[0], n_p, t_p), kv5(kv_p[1], n_p, t_p), jnp.stack(pp, axis=1), gla_p,
            kv5(kv_s[0], n_s, t_s), kv5(kv_s[1], n_s, t_s), jnp.stack(psm, axis=1), gla_s)
```

```python
import functools
import math

import numpy as np
import jax
import jax.numpy as jnp
from jax import lax
from jax.experimental import pallas as pl
from jax.experimental.pallas import tpu as pltpu

F32 = jnp.float32
BF16 = jnp.bfloat16

LANES = 128
SUBLANES = 8
VMEM_LIMIT_BYTES = 56 * 1024 * 1024

D_MODEL = 1024
DEPTH = 4
POOL_WIDTH = 512
POOL_WINDOWS = (2, 4, 8, 16)
POOL_GROUP_DIM = 128
POOL_HIST = 16
DA_HEADS = 4
DA_QK_DIM = 64
DA_V_DIM = 128
DA_WIDTH = 512
ROT_DIM = 16
ROPE_THETA = 500000.0
GLA_HEADS = 4
GLA_K_DIM = 128
GLA_V_DIM = 256
GLA_GATE_RANK = 16
GLA_TAU = 16.0
GLA_BLOCK = 64
DN_ALPHA = (2 * DEPTH) ** 0.25
LN_EPS = 1e-5
RMS_EPS = 1e-5

_NT = (((1,), (1,)), ((), ()))
_TN = (((0,), (0,)), ((), ()))


def _dot(a, b):
    return jnp.dot(a, b, preferred_element_type=F32)


def _dot_nt(a, b):
    return lax.dot_general(a, b, _NT, preferred_element_type=F32)


def _dot_tn(a, b):
    return lax.dot_general(a, b, _TN, preferred_element_type=F32)


def _silu(x):
    return x * jax.nn.sigmoid(x)


def _cparams(sem):
    return pltpu.CompilerParams(dimension_semantics=sem, vmem_limit_bytes=VMEM_LIMIT_BYTES)


def _layer_norm_rows(z, g, b):
    mu = jnp.mean(z, axis=-1, keepdims=True)
    zc = z - mu
    var = jnp.mean(zc * zc, axis=-1, keepdims=True)
    return zc * lax.rsqrt(var + LN_EPS) * g + b


def _store_head_rows(out_ref, val, h, tn, tr):
    for s in range(tn):
        out_ref[s, pl.ds(h, tr, stride=DA_HEADS), :] = val[s * tr:(s + 1) * tr, :]


def _even_in_kernel(*refs, tn, tr, tk, pos0, has_past, n_alias):
    refs = refs[:7 + 1] + refs[7 + 1 + n_alias:]
    if has_past:
        (x_ref, w_ref, wlin_ref, pscale_ref, cos_ref, sina_ref, sinb_ref, buf_ref,
         pmix_ref, q_ref, k32_ref, v32_ref, ga_ref, npool_ref, ext_ref) = refs
        wvt_ref = kbf_ref = vt_ref = None
    else:
        (x_ref, w_ref, wlin_ref, pscale_ref, cos_ref, sina_ref, sinb_ref, wvt_ref,
         pmix_ref, q_ref, k32_ref, v32_ref, ga_ref, npool_ref, kbf_ref, vt_ref, ext_ref) = refs
        buf_ref = None
    r = pl.program_id(1)
    rows = tn * tr
    xb = x_ref[...].astype(BF16)

    def proj(c0, n):
        return _dot(xb, w_ref[:, c0:c0 + n])

    pu = proj(0, POOL_WIDTH)
    pg = proj(POOL_WIDTH, POOL_WIDTH)
    if has_past:
        ext_ref[:, 0:POOL_HIST, :] = buf_ref[...]
    else:
        @pl.when(r == 0)
        def _():
            ext_ref[:, 0:POOL_HIST, :] = jnp.zeros((tn, POOL_HIST, POOL_WIDTH), F32)

        @pl.when(r > 0)
        def _():
            ext_ref[:, 0:POOL_HIST, :] = ext_ref[:, tr:tr + POOL_HIST, :]
    for s in range(tn):
        ext_ref[s, POOL_HIST:POOL_HIST + tr, :] = pu[s * tr:(s + 1) * tr, :]
    npool_ref[...] = ext_ref[:, tr:tr + POOL_HIST, :]

    pos = pos0 + r * tr + lax.broadcasted_iota(jnp.int32, (tr, POOL_GROUP_DIM), 0)
    for g, w in enumerate(POOL_WINDOWS):
        c0 = g * POOL_GROUP_DIM
        cnt = jnp.minimum(pos + 1, w).astype(F32)
        pieces = []
        for s in range(tn):
            u_g = ext_ref[s, POOL_HIST:POOL_HIST + tr, c0:c0 + POOL_GROUP_DIM]
            acc = u_g
            for i in range(1, w):
                acc = acc + ext_ref[s, POOL_HIST - i:POOL_HIST - i + tr, c0:c0 + POOL_GROUP_DIM]
            pieces.append(acc / cnt - u_g)
        d = pieces[0] if tn == 1 else jnp.concatenate(pieces, axis=0)
        y = _dot(d.astype(BF16), wlin_ref[g]) * pscale_ref[:, c0:c0 + POOL_GROUP_DIM]
        pmix_ref[:, c0:c0 + POOL_GROUP_DIM] = (y * _silu(pg[:, c0:c0 + POOL_GROUP_DIM])).astype(pmix_ref.dtype)

    cos = cos_ref[...]
    sina = sina_ref[...]
    sinb = sinb_ref[...]

    def rope(xh):
        return (xh * cos + pltpu.roll(xh, ROT_DIM // 2, 1) * sina
                + pltpu.roll(xh, LANES - ROT_DIM // 2, 1) * sinb)

    q = proj(2 * POOL_WIDTH, DA_WIDTH)
    k = proj(2 * POOL_WIDTH + DA_WIDTH, DA_WIDTH)
    for h in range(DA_HEADS):
        sl = slice(h * LANES, (h + 1) * LANES)
        q_scale = DA_QK_DIM ** -0.5 * (1.0 if has_past else math.log2(math.e))
        q_ref[:, sl] = (rope(q[:, sl]) * q_scale).astype(q_ref.dtype)
        kr = rope(k[:, sl])
        _store_head_rows(k32_ref, kr, h, tn, tr)
        if kbf_ref is not None:
            kbf_ref[:, sl] = kr.astype(kbf_ref.dtype)
    v = proj(2 * POOL_WIDTH + 2 * DA_WIDTH, DA_WIDTH)
    for h in range(DA_HEADS):
        _store_head_rows(v32_ref, v[:, h * LANES:(h + 1) * LANES], h, tn, tr)
    if vt_ref is not None:
        vt = _dot_nt(wvt_ref[...], xb).astype(vt_ref.dtype)
        for h in range(DA_HEADS):
            for c in range(tr // tk):
                vt_ref[h, c] = vt[h * DA_V_DIM:(h + 1) * DA_V_DIM, c * tk:(c + 1) * tk]
    ga_ref[...] = _silu(proj(2 * POOL_WIDTH + 3 * DA_WIDTH, DA_WIDTH)).astype(ga_ref.dtype)


def _even_in(x2d, n_seq, t_seq, tn, tr, tk, pos0, w_in, wlin, pscale, cos, sina, sinb, w_vt, buf16, layer_j,
             n_even, kv_prev, act_dtype):
    n_row_tiles = t_seq // tr
    n_seq_tiles = n_seq // tn
    rows = tn * tr
    total = n_seq * t_seq
    has_past = buf16 is not None
    n_alias = 0 if kv_prev is None else 2
    kern = functools.partial(_even_in_kernel, tn=tn, tr=tr, tk=tk, pos0=pos0, has_past=has_past, n_alias=n_alias)
    row_map = lambda n, r: (n * n_row_tiles + r, 0)
    const2 = lambda n, r: (0, 0)
    tab_map = (lambda n, r: (r, 0)) if not has_past else const2
    in_specs = [
        pl.BlockSpec((rows, D_MODEL), row_map),
        pl.BlockSpec(w_in.shape, const2),
        pl.BlockSpec(wlin.shape, lambda n, r: (0, 0, 0)),
        pl.BlockSpec(pscale.shape, const2),
        pl.BlockSpec((rows, LANES), tab_map),
        pl.BlockSpec((rows, LANES), tab_map),
        pl.BlockSpec((rows, LANES), tab_map),
    ]
    args = [x2d, w_in, wlin, pscale, cos, sina, sinb]
    blk = lambda: pl.BlockSpec((rows, DA_WIDTH), row_map)
    sds = lambda dt: jax.ShapeDtypeStruct((total, DA_WIDTH), dt)
    kv_blk = lambda: pl.BlockSpec((tn, None, tr * DA_HEADS, LANES), lambda n, r: (n, layer_j, r, 0))
    kv_sds = jax.ShapeDtypeStruct((n_seq, n_even, t_seq * DA_HEADS, LANES), F32)
    out_specs = [blk(), blk(), kv_blk(), kv_blk(), blk(),
                 pl.BlockSpec((tn, POOL_HIST, POOL_WIDTH), lambda n, r: (n, 0, 0))]
    out_shape = [sds(act_dtype), sds(act_dtype), kv_sds, kv_sds, sds(act_dtype),
                 jax.ShapeDtypeStruct((n_seq, POOL_HIST, POOL_WIDTH), F32)]
    if has_past:
        in_specs.append(pl.BlockSpec((tn, None, POOL_HIST, POOL_WIDTH), lambda n, r: (n, layer_j, 0, 0)))
        args.append(buf16)
    else:
        in_specs.append(pl.BlockSpec(w_vt.shape, const2))
        args.append(w_vt)
        out_specs += [blk(), pl.BlockSpec((None, DA_HEADS, tr // tk, DA_V_DIM, tk), lambda n, r: (n, 0, r, 0, 0))]
        out_shape += [sds(BF16), jax.ShapeDtypeStruct((n_seq, DA_HEADS, t_seq // tk, DA_V_DIM, tk), BF16)]
    aliases = {}
    if kv_prev is not None:
        aliases = {len(args): 2, len(args) + 1: 3}
        in_specs += [pl.BlockSpec(memory_space=pl.ANY), pl.BlockSpec(memory_space=pl.ANY)]
        args += list(kv_prev)
    return pl.pallas_call(
        kern,
        grid=(n_seq_tiles, n_row_tiles),
        in_specs=in_specs,
        out_specs=out_specs,
        out_shape=out_shape,
        input_output_aliases=aliases,
        scratch_shapes=[pltpu.VMEM((tn, POOL_HIST + tr, POOL_WIDTH), F32)],
        compiler_params=_cparams(("arbitrary", "arbitrary")),
        name="even_in",
    )(*args)


def _lambda_full(lam_ref, lam_init):
    lam = lam_ref[...]
    a = jnp.sum(lam[0:1, :] * lam[1:2, :], axis=1, keepdims=True)
    b = jnp.sum(lam[2:3, :] * lam[3:4, :], axis=1, keepdims=True)
    return jnp.exp(a) - jnp.exp(b) + lam_init


def _diff_finish(o1, o2, lam_full, lam_init, sub_w, gate):
    o = o1 - lam_full * o2
    ms = jnp.mean(o * o, axis=-1, keepdims=True)
    o = o * lax.rsqrt(ms + RMS_EPS) * sub_w * (1.0 - lam_init)
    return o * gate


def _stack_maps(q):
    lane = lax.broadcasted_iota(jnp.int32, q.shape, 1)
    zero = jnp.zeros_like(q)
    return jnp.concatenate([jnp.where(lane < DA_QK_DIM, q, zero), jnp.where(lane >= DA_QK_DIM, q, zero)], axis=0)


def _attn_prompt_kernel(q_ref, k_ref, vt_ref, ga_ref, lam_ref, sub_ref, o_ref, acc_ref, sa_ref, sb_ref,
                        *, tq, lam_init):
    qi = pl.program_id(2)
    qs = _stack_maps(q_ref[...])
    acc_ref[...] = jnp.zeros(acc_ref.shape, F32)

    def scores(ki, s_ref, diagonal=False):
        start = pl.multiple_of(ki * tq, tq)
        s = _dot_nt(k_ref[pl.ds(start, tq), :], qs)
        if diagonal:
            key = lax.broadcasted_iota(jnp.int32, s.shape, 0)
            qry = lax.broadcasted_iota(jnp.int32, s.shape, 1)
            qry = jnp.where(qry >= tq, qry - tq, qry)
            s = jnp.where(key <= qry, s, -jnp.inf)
        s_ref[...] = s
        return jnp.max(s, axis=0, keepdims=True)

    def absorb(ki, s_ref, mx, m_prev, l_prev):
        m_new = jnp.maximum(m_prev, mx)
        alpha = jnp.exp2(m_prev - m_new)
        p = jnp.exp2(s_ref[...] - m_new)
        l_new = alpha * l_prev + jnp.sum(p, axis=0, keepdims=True)
        acc_ref[...] = alpha * acc_ref[...] + _dot(vt_ref[ki], p.astype(BF16))
        return m_new, l_new

    n_pairs = qi // 2
    mx0 = scores(qi, sa_ref, diagonal=True)

    def pair(i, c):
        m, l, mxa = c
        mxb = scores(2 * i, sb_ref)
        m, l = absorb(jnp.where(i == 0, qi, 2 * i - 1), sa_ref, mxa, m, l)
        mxa = scores(2 * i + 1, sa_ref)
        m, l = absorb(2 * i, sb_ref, mxb, m, l)
        return m, l, mxa

    m0 = jnp.full((1, 2 * tq), -jnp.inf, F32)
    l0 = jnp.zeros((1, 2 * tq), F32)
    m, l, mxa = lax.fori_loop(0, n_pairs, pair, (m0, l0, mx0))
    in_sa = jnp.where(n_pairs == 0, qi, 2 * n_pairs - 1)

    def tail_even(m, l, mxa):
        return absorb(in_sa, sa_ref, mxa, m, l)

    def tail_odd(m, l, mxa):
        mxb = scores(qi - 1, sb_ref)
        m, l = absorb(in_sa, sa_ref, mxa, m, l)
        return absorb(qi - 1, sb_ref, mxb, m, l)

    m, l = lax.cond(qi % 2 == 0, tail_even, tail_odd, m, l, mxa)

    o_t = acc_ref[...] / l
    o1 = o_t[:, :tq].T
    o2 = o_t[:, tq:].T
    out = _diff_finish(o1, o2, _lambda_full(lam_ref, lam_init), lam_init, sub_ref[...], ga_ref[...].astype(F32))
    o_ref[...] = out.astype(o_ref.dtype)


def _attn_prompt(q, kbf, vt, ga, lam, sub_w, n_seq, t_seq, tq, lam_init):
    nq = t_seq // tq
    kern = functools.partial(_attn_prompt_kernel, tq=tq, lam_init=lam_init)
    qmap = lambda b, h, i: (b * nq + i, h)
    c2 = lambda b, h, i: (0, 0)
    return pl.pallas_call(
        kern,
        grid=(n_seq, DA_HEADS, nq),
        in_specs=[
            pl.BlockSpec((tq, LANES), qmap),
            pl.BlockSpec((t_seq, LANES), lambda b, h, i: (b, h)),
            pl.BlockSpec((None, None, nq, DA_V_DIM, tq), lambda b, h, i: (b, h, 0, 0, 0)),
            pl.BlockSpec((tq, LANES), qmap),
            pl.BlockSpec(lam.shape, c2),
            pl.BlockSpec(sub_w.shape, c2),
        ],
        out_specs=pl.BlockSpec((tq, LANES), qmap),
        out_shape=jax.ShapeDtypeStruct((n_seq * t_seq, DA_WIDTH), BF16),
        scratch_shapes=[pltpu.VMEM((DA_V_DIM, 2 * tq), F32), pltpu.VMEM((tq, 2 * tq), F32),
                        pltpu.VMEM((tq, 2 * tq), F32)],
        compiler_params=_cparams(("arbitrary", "arbitrary", "arbitrary")),
        name="attn_prompt",
    )(q, kbf, vt, ga, lam, sub_w)


def _attn_sample_kernel(*refs, t_new, n_pages, lam_init):
    pt_ref, q_ref = refs[0], refs[1]
    k_refs = refs[2:2 + n_pages]
    v_refs = refs[2 + n_pages:2 + 2 * n_pages]
    (kn_ref, vn_ref, ga_ref, lam_ref, sub_ref, bias_ref, biasn_ref, o_ref, s_ref) = refs[2 + 2 * n_pages:]
    del pt_ref
    n_rows = 2 * DA_HEADS * t_new

    pieces = [None] * (2 * DA_HEADS)
    for h in range(DA_HEADS):
        st = _stack_maps(q_ref[:, h * LANES:(h + 1) * LANES])
        pieces[h] = st[:t_new]
        pieces[DA_HEADS + h] = st[t_new:]
    wq = jnp.concatenate(pieces, axis=0).astype(BF16)

    bias = bias_ref[...]
    mx = None
    for p in range(n_pages):
        s = _dot_nt(wq, k_refs[p][...].astype(BF16)) + bias
        s_ref[p] = s
        mx = s if mx is None else jnp.maximum(mx, s)
    kn = kn_ref[...]
    vn = vn_ref[...]
    sn = _dot_nt(wq, kn.astype(BF16)) + biasn_ref[...]
    m = jnp.maximum(jnp.max(mx, axis=1, keepdims=True), jnp.max(sn, axis=1, keepdims=True))

    pn = jnp.exp(sn - m)
    acc = _dot(pn.astype(BF16), vn.astype(BF16))
    lsum = None
    for p in range(n_pages):
        pe = jnp.exp(s_ref[p] - m)
        lsum = pe if lsum is None else lsum + pe
        acc = acc + _dot(pe.astype(BF16), v_refs[p][...].astype(BF16))
    l = jnp.sum(lsum, axis=1, keepdims=True) + jnp.sum(pn, axis=1, keepdims=True)

    half = n_rows // 2
    o1 = acc[:half] / l[:half]
    o2 = acc[half:] / l[half:]
    gate = jnp.concatenate([ga_ref[:, h * LANES:(h + 1) * LANES] for h in range(DA_HEADS)], axis=0)
    out = _diff_finish(o1, o2, _lambda_full(lam_ref, lam_init), lam_init, sub_ref[...], gate)
    for h in range(DA_HEADS):
        o_ref[:, h * LANES:(h + 1) * LANES] = out[h * t_new:(h + 1) * t_new].astype(o_ref.dtype)


def _sample_biases(t_new, page_rows):
    n_rows = 2 * DA_HEADS * t_new
    row = np.arange(n_rows)[:, None]
    row_head, row_q = (row // t_new) % DA_HEADS, row % t_new
    col = np.arange(page_rows)[None, :]
    bias = np.where(row_head == col % DA_HEADS, 0.0, -np.inf).astype(np.float32)
    coln = np.arange(DA_HEADS * t_new)[None, :]
    biasn = np.where((row_head == coln % DA_HEADS) & (coln // DA_HEADS <= row_q), 0.0, -np.inf).astype(np.float32)
    return jnp.asarray(bias), jnp.asarray(biasn)


def _attn_sample(page_table_flat, q, ck, cv, k_new, v_new, ga, lam, sub_w, layer_j, n_seq, t_new, n_pages,
                 lam_init):
    page_rows = ck.shape[2]
    kern = functools.partial(_attn_sample_kernel, t_new=t_new, n_pages=n_pages, lam_init=lam_init)
    bias, biasn = _sample_biases(t_new, page_rows)
    seq_map = lambda n, pt: (n, 0)
    const2 = lambda n, pt: (0, 0)
    n_rows = 2 * DA_HEADS * t_new

    def page_spec(p):
        return pl.BlockSpec((None, None, page_rows, LANES), lambda n, pt: (pt[n * n_pages + p], layer_j, 0, 0))

    seq_spec = lambda: pl.BlockSpec((t_new, DA_WIDTH), seq_map)
    new_spec = lambda: pl.BlockSpec((None, None, t_new * DA_HEADS, LANES), lambda n, pt: (n, layer_j, 0, 0))
    grid_spec = pltpu.PrefetchScalarGridSpec(
        num_scalar_prefetch=1,
        grid=(n_seq,),
        in_specs=([seq_spec()] + [page_spec(p) for p in range(n_pages)] + [page_spec(p) for p in range(n_pages)]
                  + [new_spec(), new_spec(), seq_spec(), pl.BlockSpec(lam.shape, const2),
                     pl.BlockSpec(sub_w.shape, const2), pl.BlockSpec(bias.shape, const2),
                     pl.BlockSpec(biasn.shape, const2)]),
        out_specs=seq_spec(),
        scratch_shapes=[pltpu.VMEM((n_pages, n_rows, page_rows), F32)],
    )
    return pl.pallas_call(
        kern,
        grid_spec=grid_spec,
        out_shape=jax.ShapeDtypeStruct((n_seq * t_new, DA_WIDTH), F32),
        compiler_params=_cparams(("arbitrary",)),
        name="attn_sample",
    )(page_table_flat, q, *([ck] * n_pages), *([cv] * n_pages), k_new, v_new, ga, lam, sub_w, bias, biasn)


def _out_ln_kernel(a_ref, b_ref, x_ref, w_ref, g_ref, bb_ref, y_ref):
    half = a_ref.shape[1]
    acc = _dot(a_ref[...].astype(BF16), w_ref[0:half, :]) + _dot(b_ref[...].astype(BF16), w_ref[half:2 * half, :])
    y_ref[...] = _layer_norm_rows(DN_ALPHA * x_ref[...] + acc, g_ref[...], bb_ref[...])


def _out_ln(a, b, x2d, w_out, ln_g, ln_b, tm):
    total = x2d.shape[0]
    row = lambda i: (i, 0)
    const = lambda i: (0, 0)
    return pl.pallas_call(
        _out_ln_kernel,
        grid=(total // tm,),
        in_specs=[
            pl.BlockSpec((tm, a.shape[1]), row),
            pl.BlockSpec((tm, b.shape[1]), row),
            pl.BlockSpec((tm, D_MODEL), row),
            pl.BlockSpec(w_out.shape, const),
            pl.BlockSpec(ln_g.shape, const),
            pl.BlockSpec(ln_b.shape, const),
        ],
        out_specs=pl.BlockSpec((tm, D_MODEL), row),
        out_shape=jax.ShapeDtypeStruct((total, D_MODEL), F32),
        compiler_params=_cparams(("arbitrary",)),
        name="out_ln",
    )(a, b, x2d, w_out, ln_g, ln_b)


def _gla_constants(top):
    big = GLA_BLOCK
    idx = np.arange(big)
    levels = int(round(math.log2(top)))
    seg = [((idx[:, None] // top == idx[None, :] // top) & (idx[None, :] <= idx[:, None])).astype(np.float32)]
    pairs = []
    n = top
    for _ in range(levels):
        start = (idx // n) * n
        rho = start + n // 2 - 1
        upper = idx > rho
        t = idx[None, :]
        c = np.where(upper[:, None], (t > rho[:, None]) & (t <= idx[:, None]), (t > idx[:, None]) & (t <= rho[:, None]))
        seg.append(c.astype(np.float32))
        pairs.append(((start[:, None] == start[None, :]) & upper[:, None] & ~upper[None, :]).astype(np.float32))
        n //= 2
    seg = np.concatenate(seg, axis=0)
    return jnp.asarray(np.concatenate([seg, seg], axis=1), BF16), jnp.asarray(np.stack(pairs)), levels


def _gla_intra(q_h, k_h, seg_h, pair_ref, levels):
    big = GLA_BLOCK
    att = jnp.zeros((big, big), F32)
    for lv in range(levels):
        e = jnp.exp2(seg_h[(1 + lv) * big:(2 + lv) * big, :])
        att = att + _dot_nt((q_h * e).astype(BF16), (k_h * e).astype(BF16)) * pair_ref[lv]
    diag = jnp.sum(q_h * k_h, axis=1, keepdims=True)
    row = lax.broadcasted_iota(jnp.int32, (big, big), 0)
    col = lax.broadcasted_iota(jnp.int32, (big, big), 1)
    return att + jnp.where(row == col, diag, 0.0)


def _odd_kernel(*refs, tn, tr, levels, has_state, n_alias):
    n_in = 11 + (1 if has_state else 0)
    refs = refs[:n_in] + refs[n_in + n_alias:]
    if has_state:
        (x_ref, w_ref, wga_ref, wgb_ref, bg_ref, nw_ref, wo_ref, lng_ref, lnb_ref, seg_ref, pair_ref,
         s0_ref, y_ref, sout_ref, q_s, k_s, v_s, g_s, o_s, st_s) = refs
    else:
        (x_ref, w_ref, wga_ref, wgb_ref, bg_ref, nw_ref, wo_ref, lng_ref, lnb_ref, seg_ref, pair_ref,
         y_ref, sout_ref, q_s, k_s, v_s, g_s, o_s, st_s) = refs
        s0_ref = None
    r = pl.program_id(1)
    rows = tn * tr
    big = GLA_BLOCK
    kw = GLA_HEADS * GLA_K_DIM
    vw = GLA_HEADS * GLA_V_DIM
    x = x_ref[...]
    xb = x.astype(BF16)
    q_s[...] = _dot(xb, w_ref[:, 0:kw]) * (GLA_K_DIM ** -0.5)
    k_s[...] = _dot(xb, w_ref[:, kw:2 * kw])
    v_s[...] = _dot(xb, w_ref[:, 2 * kw:2 * kw + vw]).astype(BF16)
    low = _dot(xb, wga_ref[...])
    z = _dot(low.astype(BF16), wgb_ref[...]) + bg_ref[...]
    g_s[...] = (jnp.minimum(z, 0.0) - jnp.log1p(jnp.exp(-jnp.abs(z)))) * (math.log2(math.e) / GLA_TAU)

    if not has_state:
        @pl.when(r == 0)
        def _():
            st_s[...] = jnp.zeros(st_s.shape, F32)

    def block(c, carry):
        r0 = pl.multiple_of(c * big, big)
        qb = q_s[pl.ds(r0, big), :]
        kb = k_s[pl.ds(r0, big), :]
        gb = g_s[pl.ds(r0, big), :]
        hi = gb.astype(BF16)
        lo = (gb - hi.astype(F32)).astype(BF16)
        seg = _dot(seg_ref[...], jnp.concatenate([hi, lo], axis=0))
        for h in range(GLA_HEADS):
            ks = slice(h * GLA_K_DIM, (h + 1) * GLA_K_DIM)
            q_h, k_h, seg_h = qb[:, ks], kb[:, ks], seg[:, ks]
            v_h = v_s[pl.ds(r0, big), h * GLA_V_DIM:(h + 1) * GLA_V_DIM]
            att = _gla_intra(q_h, k_h, seg_h, pair_ref, levels)
            o_h = _dot(att.astype(BF16), v_h)
            cb = seg_h[0:big, :]
            qe = q_h * jnp.exp2(cb)
            if has_state:
                n_runs = big // tr
                v_hf = v_h.astype(F32)
                pad_k = jnp.zeros((2 * SUBLANES - tr, GLA_K_DIM), F32)
                pad_v = jnp.zeros((2 * SUBLANES - tr, GLA_V_DIM), F32)
                inter = []
                for s in range(n_runs):
                    rs = slice(s * tr, (s + 1) * tr)
                    seq = c * n_runs + s
                    st = s0_ref[seq, h]
                    qep = jnp.concatenate([qe[rs], pad_k], axis=0).astype(BF16)
                    inter.append(_dot(qep, st.astype(BF16))[0:tr])
                    cb_last = cb[(s + 1) * tr - 1:(s + 1) * tr, :]
                    kt = k_h[rs] * jnp.exp2(cb_last - cb[rs])
                    ktp = jnp.concatenate([kt, pad_k], axis=0).astype(BF16)
                    vtp = jnp.concatenate([v_hf[rs], pad_v], axis=0).astype(BF16)
                    dcol = jnp.broadcast_to(jnp.exp2(cb_last), (GLA_K_DIM, GLA_K_DIM)).T
                    decay = jnp.concatenate([dcol] * (GLA_V_DIM // GLA_K_DIM), axis=1)
                    sout_ref[seq, h] = st * decay + _dot_tn(ktp, vtp)
                o_h = o_h + jnp.concatenate(inter, axis=0)
            else:
                st = st_s[h]
                o_h = o_h + _dot_nt(qe.astype(BF16), st.astype(BF16))
                cb_last = cb[big - 1:big, :]
                kt = (k_h * jnp.exp2(cb_last - cb)).astype(BF16)
                st_s[h] = st * jnp.exp2(cb_last) + _dot_tn(v_h, kt)
            o_s[pl.ds(r0, big), h * GLA_V_DIM:(h + 1) * GLA_V_DIM] = o_h
        return carry

    lax.fori_loop(0, rows // big, block, 0, unroll=4 if rows // big >= 4 else 1)

    if not has_state:
        @pl.when(r == pl.num_programs(1) - 1)
        def _():
            for h in range(GLA_HEADS):
                sout_ref[0, h] = st_s[h].T

    nw = nw_ref[...]
    for h in range(GLA_HEADS):
        vs = slice(h * GLA_V_DIM, (h + 1) * GLA_V_DIM)
        o_h = o_s[:, vs]
        ms = jnp.mean(o_h * o_h, axis=-1, keepdims=True)
        gate = _dot(xb, w_ref[:, 2 * kw + vw + h * GLA_V_DIM:2 * kw + vw + (h + 1) * GLA_V_DIM])
        o_s[:, vs] = o_h * lax.rsqrt(ms + RMS_EPS) * nw * _silu(gate)
    acc = _dot(o_s[...].astype(BF16), wo_ref[...])
    y_ref[...] = _layer_norm_rows(DN_ALPHA * x + acc, lng_ref[...], lnb_ref[...])


def _odd_layer(x2d, n_seq, t_seq, tn, tr, w_in, wga, wgb, bg, nw, wo, lng, lnb, state0, layer_j, n_odd,
               state_prev):
    has_state = state0 is not None
    seg, pair, levels = _gla_constants(tr if has_state else GLA_BLOCK)
    n_row_tiles = t_seq // tr
    rows = tn * tr
    total = n_seq * t_seq
    n_alias = 0 if state_prev is None else 1
    kern = functools.partial(_odd_kernel, tn=tn, tr=tr, levels=levels, has_state=has_state, n_alias=n_alias)
    row_map = lambda n, r: (n * n_row_tiles + r, 0)
    c2 = lambda n, r: (0, 0)
    c3 = lambda n, r: (0, 0, 0)
    st_map = lambda n, r: (n, layer_j, 0, 0, 0)
    in_specs = [
        pl.BlockSpec((rows, D_MODEL), row_map),
        pl.BlockSpec(w_in.shape, c2), pl.BlockSpec(wga.shape, c2), pl.BlockSpec(wgb.shape, c2),
        pl.BlockSpec(bg.shape, c2), pl.BlockSpec(nw.shape, c2), pl.BlockSpec(wo.shape, c2),
        pl.BlockSpec(lng.shape, c2), pl.BlockSpec(lnb.shape, c2),
        pl.BlockSpec(seg.shape, c2), pl.BlockSpec(pair.shape, c3),
    ]
    args = [x2d, w_in, wga, wgb, bg, nw, wo, lng, lnb, seg, pair]
    st_block = (tn, None, GLA_HEADS, GLA_K_DIM, GLA_V_DIM)
    if has_state:
        in_specs.append(pl.BlockSpec(st_block, st_map))
        args.append(state0)
    aliases = {}
    if state_prev is not None:
        aliases = {len(args): 1}
        in_specs.append(pl.BlockSpec(memory_space=pl.ANY))
        args.append(state_prev)
    kw = GLA_HEADS * GLA_K_DIM
    vw = GLA_HEADS * GLA_V_DIM
    return pl.pallas_call(
        kern,
        grid=(n_seq // tn, n_row_tiles),
        in_specs=in_specs,
        out_specs=[pl.BlockSpec((rows, D_MODEL), row_map), pl.BlockSpec(st_block, st_map)],
        out_shape=[jax.ShapeDtypeStruct((total, D_MODEL), F32),
                   jax.ShapeDtypeStruct((n_seq, n_odd, GLA_HEADS, GLA_K_DIM, GLA_V_DIM), F32)],
        input_output_aliases=aliases,
        scratch_shapes=[pltpu.VMEM((rows, kw), F32), pltpu.VMEM((rows, kw), F32), pltpu.VMEM((rows, vw), BF16),
                        pltpu.VMEM((rows, kw), F32), pltpu.VMEM((rows, vw), F32),
                        pltpu.VMEM((GLA_HEADS, GLA_V_DIM, GLA_K_DIM), F32)],
        compiler_params=_cparams(("arbitrary", "arbitrary")),
        name="odd_layer",
    )(*args)


def _rope_tables(positions):
    half = ROT_DIM // 2
    inv_freq = jnp.power(jnp.float32(ROPE_THETA), -jnp.arange(0, ROT_DIM, 2, dtype=jnp.float32) / ROT_DIM)
    ang = positions.astype(jnp.float32)[:, None] * inv_freq[None, :]
    cos, sin = jnp.cos(ang), jnp.sin(ang)
    n = positions.shape[0]
    ones = jnp.ones((n, DA_QK_DIM - ROT_DIM), F32)
    zeros = jnp.zeros((n, DA_QK_DIM - ROT_DIM), F32)
    zh = jnp.zeros((n, half), F32)
    cos_map = jnp.concatenate([cos, cos, ones], axis=1)
    sina_map = jnp.concatenate([zh, sin, zeros], axis=1)
    sinb_map = jnp.concatenate([-sin, zh, zeros], axis=1)
    two = lambda t: jnp.concatenate([t, t], axis=1)
    return two(cos_map), two(sina_map), two(sinb_map)


def kernel(x_prompt, x_sample, cache_k, cache_v, state_pool, state_gla, page_table, w_in_even, w_pool_lin,
           pool_scale, diff_lambda_params, subln_w, w_out_even, w_in_odd, w_gate_a, w_gate_b, b_gate, gla_norm_w,
           w_out_odd, ln_g, ln_b):
    n_p, t_p, _ = x_prompt.shape
    n_s, t_s, _ = x_sample.shape
    n_pool, n_even, page_size = cache_k.shape[:3]
    n_pages = page_table.shape[1]
    past_len = n_pages * page_size

    tr_p = min(512, t_p)
    tq_p = min(512, t_p)
    tn_even_s = 16
    tn_odd_s = GLA_BLOCK // t_s

    w_in_even_b = w_in_even.astype(BF16)
    w_pool_b = w_pool_lin.astype(BF16)
    w_out_even_b = w_out_even.astype(BF16)
    w_in_odd_b = w_in_odd.astype(BF16)
    w_out_odd_b = w_out_odd.astype(BF16)
    pad_rank = LANES - GLA_GATE_RANK
    w_ga_b = jnp.pad(w_gate_a, ((0, 0), (0, 0), (0, pad_rank))).astype(BF16)
    w_gb_b = jnp.pad(w_gate_b, ((0, 0), (0, pad_rank), (0, 0))).astype(BF16)

    cos_p, sina_p, sinb_p = _rope_tables(jnp.arange(t_p))
    cos_s, sina_s, sinb_s = (jnp.tile(t, (tn_even_s, 1)) for t in _rope_tables(past_len + jnp.arange(t_s)))

    ck = cache_k.reshape(n_pool, n_even, page_size * DA_HEADS, 2 * DA_QK_DIM)
    cv = cache_v.reshape(n_pool, n_even, page_size * DA_HEADS, DA_V_DIM)
    pt_flat = page_table.reshape(-1).astype(jnp.int32)
    pool16 = jnp.pad(state_pool, ((0, 0), (0, 0), (POOL_HIST - state_pool.shape[2], 0), (0, 0)))

    yp = x_prompt.reshape(n_p * t_p, D_MODEL)
    ys = x_sample.reshape(n_s * t_s, D_MODEL)
    n_odd = state_gla.shape[1]
    pp, psm = [], []
    kv_p = tuple(jnp.zeros((n_p, n_even, t_p * DA_HEADS, LANES), F32) for _ in range(2))
    kv_s = tuple(jnp.zeros((n_s, n_even, t_s * DA_HEADS, LANES), F32) for _ in range(2))
    gla_p = jnp.zeros((n_p, n_odd) + state_gla.shape[2:], F32)
    gla_s = jnp.zeros((n_s, n_odd) + state_gla.shape[2:], F32)
    for i in range(DEPTH):
        j = i // 2
        lng, lnb = ln_g[i][None, :], ln_b[i][None, :]
        if i % 2 == 0:
            lam_init = 0.8 - 0.6 * math.exp(-0.3 * i)
            pscale = pool_scale[j][None, :]
            sub_w = subln_w[j][None, :]
            lam = diff_lambda_params[j]
            w_vt = w_in_even_b[j][:, 2 * POOL_WIDTH + 2 * DA_WIDTH:2 * POOL_WIDTH + 3 * DA_WIDTH].T
            pmix, q, k32, v32, ga, npool, kbf, vt = _even_in(
                yp, n_p, t_p, 1, tr_p, tq_p, 0, w_in_even_b[j], w_pool_b[j], pscale, cos_p, sina_p, sinb_p, w_vt,
                None, j, n_even, kv_p, BF16)
            kv_p = (k32, v32)
            omix = _attn_prompt(q, kbf, vt, ga, lam, sub_w, n_p, t_p, tq_p, lam_init)
            yp = _out_ln(pmix, omix, yp, w_out_even_b[j], lng, lnb, tr_p)
            pp.append(npool[:, 1:])
            pmix, q, k32, v32, ga, npool = _even_in(
                ys, n_s, t_s, tn_even_s, t_s, t_s, past_len, w_in_even_b[j], w_pool_b[j], pscale, cos_s, sina_s,
                sinb_s, None, pool16, j, n_even, kv_s, F32)
            kv_s = (k32, v32)
            omix = _attn_sample(pt_flat, q, ck, cv, k32, v32, ga, lam, sub_w, j, n_s, t_s, n_pages, lam_init)
            ys = _out_ln(pmix, omix, ys, w_out_even_b[j], lng, lnb, min(512, n_s * t_s))
            psm.append(npool[:, 1:])
        else:
            ow = (w_in_odd_b[j], w_ga_b[j], w_gb_b[j], b_gate[j][None, :], gla_norm_w[j][None, :], w_out_odd_b[j],
                  lng, lnb)
            yp, gla_p = _odd_layer(yp, n_p, t_p, 1, tr_p, *ow, None, j, n_odd, gla_p)
            ys, gla_s = _odd_layer(ys, n_s, t_s, tn_odd_s, t_s, *ow, state_gla, j, n_odd, gla_s)
    kv5 = lambda a, n, t: a.reshape(n, n_even, t, DA_HEADS, LANES)
    return (yp.reshape(n_p, t_p, D_MODEL), ys.reshape(n_s, t_s, D_MODEL),
            kv5(kv_p[0], n_p, t_p), kv5(kv_p[1], n_p, t_p), jnp.stack(pp, axis=1), gla_p,
            kv5(kv_s[0], n_s, t_s), kv5(kv_s[1], n_s, t_s), jnp.stack(psm, axis=1), gla_s)
```

```python
import functools
import math

import numpy as np
import jax
import jax.numpy as jnp
from jax import lax
from jax.experimental import pallas as pl
from jax.experimental.pallas import tpu as pltpu

F32 = jnp.float32
BF16 = jnp.bfloat16

LANES = 128
SUBLANES = 8
VMEM_LIMIT_BYTES = 56 * 1024 * 1024

D_MODEL = 1024
DEPTH = 4
POOL_WIDTH = 512
POOL_WINDOWS = (2, 4, 8, 16)
POOL_GROUP_DIM = 128
POOL_HIST = 16
DA_HEADS = 4
DA_QK_DIM = 64
DA_V_DIM = 128
DA_WIDTH = 512
ROT_DIM = 16
ROPE_THETA = 500000.0
GLA_HEADS = 4
GLA_K_DIM = 128
GLA_V_DIM = 256
GLA_GATE_RANK = 16
GLA_TAU = 16.0
GLA_BLOCK = 64
DN_ALPHA = (2 * DEPTH) ** 0.25
LN_EPS = 1e-5
RMS_EPS = 1e-5

_NT = (((1,), (1,)), ((), ()))
_TN = (((0,), (0,)), ((), ()))


def _dot(a, b):
    return jnp.dot(a, b, preferred_element_type=F32)


def _dot_nt(a, b):
    return lax.dot_general(a, b, _NT, preferred_element_type=F32)


def _dot_tn(a, b):
    return lax.dot_general(a, b, _TN, preferred_element_type=F32)


def _silu(x):
    return x * jax.nn.sigmoid(x)


def _cparams(sem):
    return pltpu.CompilerParams(dimension_semantics=sem, vmem_limit_bytes=VMEM_LIMIT_BYTES)


def _layer_norm_rows(z, g, b):
    mu = jnp.mean(z, axis=-1, keepdims=True)
    zc = z - mu
    var = jnp.mean(zc * zc, axis=-1, keepdims=True)
    return zc * lax.rsqrt(var + LN_EPS) * g + b


def _store_head_rows(out_ref, val, h, tn, tr):
    for s in range(tn):
        out_ref[s, pl.ds(h, tr, stride=DA_HEADS), :] = val[s * tr:(s + 1) * tr, :]


def _even_in_kernel(*refs, tn, tr, tk, pos0, has_past, n_alias):
    refs = refs[:7 + 1] + refs[7 + 1 + n_alias:]
    if has_past:
        (x_ref, w_ref, wlin_ref, pscale_ref, cos_ref, sina_ref, sinb_ref, buf_ref,
         pmix_ref, q_ref, k32_ref, v32_ref, ga_ref, npool_ref, ext_ref) = refs
        wvt_ref = kbf_ref = vt_ref = None
    else:
        (x_ref, w_ref, wlin_ref, pscale_ref, cos_ref, sina_ref, sinb_ref, wvt_ref,
         pmix_ref, q_ref, k32_ref, v32_ref, ga_ref, npool_ref, kbf_ref, vt_ref, ext_ref) = refs
        buf_ref = None
    r = pl.program_id(1)
    rows = tn * tr
    xb = x_ref[...].astype(BF16)

    def proj(c0, n):
        return _dot(xb, w_ref[:, c0:c0 + n])

    pu = proj(0, POOL_WIDTH)
    pg = proj(POOL_WIDTH, POOL_WIDTH)
    if has_past:
        ext_ref[:, 0:POOL_HIST, :] = buf_ref[...]
    else:
        @pl.when(r == 0)
        def _():
            ext_ref[:, 0:POOL_HIST, :] = jnp.zeros((tn, POOL_HIST, POOL_WIDTH), F32)

        @pl.when(r > 0)
        def _():
            ext_ref[:, 0:POOL_HIST, :] = ext_ref[:, tr:tr + POOL_HIST, :]
    for s in range(tn):
        ext_ref[s, POOL_HIST:POOL_HIST + tr, :] = pu[s * tr:(s + 1) * tr, :]
    npool_ref[...] = ext_ref[:, tr:tr + POOL_HIST, :]

    pos = pos0 + r * tr + lax.broadcasted_iota(jnp.int32, (tr, POOL_GROUP_DIM), 0)
    for g, w in enumerate(POOL_WINDOWS):
        c0 = g * POOL_GROUP_DIM
        cnt = jnp.minimum(pos + 1, w).astype(F32)
        pieces = []
        for s in range(tn):
            u_g = ext_ref[s, POOL_HIST:POOL_HIST + tr, c0:c0 + POOL_GROUP_DIM]
            acc = u_g
            for i in range(1, w):
                acc = acc + ext_ref[s, POOL_HIST - i:POOL_HIST - i + tr, c0:c0 + POOL_GROUP_DIM]
            pieces.append(acc / cnt - u_g)
        d = pieces[0] if tn == 1 else jnp.concatenate(pieces, axis=0)
        y = _dot(d.astype(BF16), wlin_ref[g]) * pscale_ref[:, c0:c0 + POOL_GROUP_DIM]
        pmix_ref[:, c0:c0 + POOL_GROUP_DIM] = (y * _silu(pg[:, c0:c0 + POOL_GROUP_DIM])).astype(pmix_ref.dtype)

    cos = cos_ref[...]
    sina = sina_ref[...]
    sinb = sinb_ref[...]

    def rope(xh):
        return (xh * cos + pltpu.roll(xh, ROT_DIM // 2, 1) * sina
                + pltpu.roll(xh, LANES - ROT_DIM // 2, 1) * sinb)

    q = proj(2 * POOL_WIDTH, DA_WIDTH)
    k = proj(2 * POOL_WIDTH + DA_WIDTH, DA_WIDTH)
    for h in range(DA_HEADS):
        sl = slice(h * LANES, (h + 1) * LANES)
        q_scale = DA_QK_DIM ** -0.5 * (1.0 if has_past else math.log2(math.e))
        q_ref[:, sl] = (rope(q[:, sl]) * q_scale).astype(q_ref.dtype)
        kr = rope(k[:, sl])
        _store_head_rows(k32_ref, kr, h, tn, tr)
        if kbf_ref is not None:
            kbf_ref[:, sl] = kr.astype(kbf_ref.dtype)
    v = proj(2 * POOL_WIDTH + 2 * DA_WIDTH, DA_WIDTH)
    for h in range(DA_HEADS):
        _store_head_rows(v32_ref, v[:, h * LANES:(h + 1) * LANES], h, tn, tr)
    if vt_ref is not None:
        vt = _dot_nt(wvt_ref[...], xb).astype(vt_ref.dtype)
        for h in range(DA_HEADS):
            for c in range(tr // tk):
                vt_ref[h, c] = vt[h * DA_V_DIM:(h + 1) * DA_V_DIM, c * tk:(c + 1) * tk]
    ga_ref[...] = _silu(proj(2 * POOL_WIDTH + 3 * DA_WIDTH, DA_WIDTH)).astype(ga_ref.dtype)


def _even_in(x2d, n_seq, t_seq, tn, tr, tk, pos0, w_in, wlin, pscale, cos, sina, sinb, w_vt, buf16, layer_j,
             n_even, kv_prev, act_dtype):
    n_row_tiles = t_seq // tr
    n_seq_tiles = n_seq // tn
    rows = tn * tr
    total = n_seq * t_seq
    has_past = buf16 is not None
    n_alias = 0 if kv_prev is None else 2
    kern = functools.partial(_even_in_kernel, tn=tn, tr=tr, tk=tk, pos0=pos0, has_past=has_past, n_alias=n_alias)
    row_map = lambda n, r: (n * n_row_tiles + r, 0)
    const2 = lambda n, r: (0, 0)
    tab_map = (lambda n, r: (r, 0)) if not has_past else const2
    in_specs = [
        pl.BlockSpec((rows, D_MODEL), row_map),
        pl.BlockSpec(w_in.shape, const2),
        pl.BlockSpec(wlin.shape, lambda n, r: (0, 0, 0)),
        pl.BlockSpec(pscale.shape, const2),
        pl.BlockSpec((rows, LANES), tab_map),
        pl.BlockSpec((rows, LANES), tab_map),
        pl.BlockSpec((rows, LANES), tab_map),
    ]
    args = [x2d, w_in, wlin, pscale, cos, sina, sinb]
    blk = lambda: pl.BlockSpec((rows, DA_WIDTH), row_map)
    sds = lambda dt: jax.ShapeDtypeStruct((total, DA_WIDTH), dt)
    kv_blk = lambda: pl.BlockSpec((tn, None, tr * DA_HEADS, LANES), lambda n, r: (n, layer_j, r, 0))
    kv_sds = jax.ShapeDtypeStruct((n_seq, n_even, t_seq * DA_HEADS, LANES), F32)
    out_specs = [blk(), blk(), kv_blk(), kv_blk(), blk(),
                 pl.BlockSpec((tn, POOL_HIST, POOL_WIDTH), lambda n, r: (n, 0, 0))]
    out_shape = [sds(act_dtype), sds(act_dtype), kv_sds, kv_sds, sds(act_dtype),
                 jax.ShapeDtypeStruct((n_seq, POOL_HIST, POOL_WIDTH), F32)]
    if has_past:
        in_specs.append(pl.BlockSpec((tn, None, POOL_HIST, POOL_WIDTH), lambda n, r: (n, layer_j, 0, 0)))
        args.append(buf16)
    else:
        in_specs.append(pl.BlockSpec(w_vt.shape, const2))
        args.append(w_vt)
        out_specs += [blk(), pl.BlockSpec((None, DA_HEADS, tr // tk, DA_V_DIM, tk), lambda n, r: (n, 0, r, 0, 0))]
        out_shape += [sds(BF16), jax.ShapeDtypeStruct((n_seq, DA_HEADS, t_seq // tk, DA_V_DIM, tk), BF16)]
    aliases = {}
    if kv_prev is not None:
        aliases = {len(args): 2, len(args) + 1: 3}
        in_specs += [pl.BlockSpec(memory_space=pl.ANY), pl.BlockSpec(memory_space=pl.ANY)]
        args += list(kv_prev)
    return pl.pallas_call(
        kern,
        grid=(n_seq_tiles, n_row_tiles),
        in_specs=in_specs,
        out_specs=out_specs,
        out_shape=out_shape,
        input_output_aliases=aliases,
        scratch_shapes=[pltpu.VMEM((tn, POOL_HIST + tr, POOL_WIDTH), F32)],
        compiler_params=_cparams(("arbitrary", "arbitrary")),
        name="even_in",
    )(*args)


def _lambda_full(lam_ref, lam_init):
    lam = lam_ref[...]
    a = jnp.sum(lam[0:1, :] * lam[1:2, :], axis=1, keepdims=True)
    b = jnp.sum(lam[2:3, :] * lam[3:4, :], axis=1, keepdims=True)
    return jnp.exp(a) - jnp.exp(b) + lam_init


def _diff_finish(o1, o2, lam_full, lam_init, sub_w, gate):
    o = o1 - lam_full * o2
    ms = jnp.mean(o * o, axis=-1, keepdims=True)
    o = o * lax.rsqrt(ms + RMS_EPS) * sub_w * (1.0 - lam_init)
    return o * gate


def _stack_maps(q):
    lane = lax.broadcasted_iota(jnp.int32, q.shape, 1)
    zero = jnp.zeros_like(q)
    return jnp.concatenate([jnp.where(lane < DA_QK_DIM, q, zero), jnp.where(lane >= DA_QK_DIM, q, zero)], axis=0)


def _attn_flat_kernel(q_ref, k_ref, vt_ref, ga_ref, lam_ref, sub_ref, o_ref, acc_ref, s_ref, qs_ref,
                      *, tq, nq, lam_init):
    steps = [(qi, kb) for qi in range(nq) for kb in [qi] + list(range(qi))]
    lam_full = _lambda_full(lam_ref, lam_init)

    def scores(n):
        qi, kb = steps[n]
        if kb == qi:
            qs_ref[qi % 2] = _stack_maps(q_ref[qi * tq:(qi + 1) * tq, :])
        s = _dot_nt(k_ref[kb * tq:(kb + 1) * tq, :], qs_ref[qi % 2])
        if kb == qi:
            key = lax.broadcasted_iota(jnp.int32, s.shape, 0)
            qry = lax.broadcasted_iota(jnp.int32, s.shape, 1)
            qry = jnp.where(qry >= tq, qry - tq, qry)
            s = jnp.where(key <= qry, s, -jnp.inf)
        s_ref[n % 2] = s
        return jnp.max(s, axis=0, keepdims=True)

    m = l = None
    mx = scores(0)
    for n, (qi, kb) in enumerate(steps):
        mx_next = scores(n + 1) if n + 1 < len(steps) else None
        slot = qi % 2
        if kb == qi:
            m = mx
            p = jnp.exp2(s_ref[n % 2] - m)
            l = jnp.sum(p, axis=0, keepdims=True)
            acc_ref[slot] = _dot(vt_ref[kb], p.astype(BF16))
        else:
            m_new = jnp.maximum(m, mx)
            alpha = jnp.exp2(m - m_new)
            p = jnp.exp2(s_ref[n % 2] - m_new)
            l = alpha * l + jnp.sum(p, axis=0, keepdims=True)
            acc_ref[slot] = alpha * acc_ref[slot] + _dot(vt_ref[kb], p.astype(BF16))
            m = m_new
        if n + 1 == len(steps) or steps[n + 1][0] != qi:
            rows = slice(qi * tq, (qi + 1) * tq)
            o_t = acc_ref[slot] / l
            out = _diff_finish(o_t[:, :tq].T, o_t[:, tq:].T, lam_full, lam_init, sub_ref[...],
                               ga_ref[rows, :].astype(F32))
            o_ref[rows, :] = out.astype(o_ref.dtype)
        mx = mx_next


def _attn_flat(q, kbf, vt, ga, lam, sub_w, n_seq, t_seq, tq, lam_init):
    nq = t_seq // tq
    kern = functools.partial(_attn_flat_kernel, tq=tq, nq=nq, lam_init=lam_init)
    seq_head = lambda b, h: (b, h)
    c2 = lambda b, h: (0, 0)
    return pl.pallas_call(
        kern,
        grid=(n_seq, DA_HEADS),
        in_specs=[
            pl.BlockSpec((t_seq, LANES), seq_head),
            pl.BlockSpec((t_seq, LANES), seq_head),
            pl.BlockSpec((None, None, nq, DA_V_DIM, tq), lambda b, h: (b, h, 0, 0, 0)),
            pl.BlockSpec((t_seq, LANES), seq_head),
            pl.BlockSpec(lam.shape, c2),
            pl.BlockSpec(sub_w.shape, c2),
        ],
        out_specs=pl.BlockSpec((t_seq, LANES), seq_head),
        out_shape=jax.ShapeDtypeStruct((n_seq * t_seq, DA_WIDTH), BF16),
        scratch_shapes=[pltpu.VMEM((2, DA_V_DIM, 2 * tq), F32), pltpu.VMEM((2, tq, 2 * tq), F32),
                        pltpu.VMEM((2, 2 * tq, LANES), BF16)],
        compiler_params=_cparams(("arbitrary", "arbitrary")),
        name="attn_prompt",
    )(q, kbf, vt, ga, lam, sub_w)


def _attn_sample_kernel(*refs, t_new, n_pages, lam_init):
    pt_ref, q_ref = refs[0], refs[1]
    k_refs = refs[2:2 + n_pages]
    v_refs = refs[2 + n_pages:2 + 2 * n_pages]
    (kn_ref, vn_ref, ga_ref, lam_ref, sub_ref, bias_ref, biasn_ref, o_ref, s_ref) = refs[2 + 2 * n_pages:]
    del pt_ref
    n_rows = 2 * DA_HEADS * t_new

    pieces = [None] * (2 * DA_HEADS)
    for h in range(DA_HEADS):
        st = _stack_maps(q_ref[:, h * LANES:(h + 1) * LANES])
        pieces[h] = st[:t_new]
        pieces[DA_HEADS + h] = st[t_new:]
    wq = jnp.concatenate(pieces, axis=0).astype(BF16)

    bias = bias_ref[...]
    mx = None
    for p in range(n_pages):
        s = _dot_nt(wq, k_refs[p][...].astype(BF16)) + bias
        s_ref[p] = s
        mx = s if mx is None else jnp.maximum(mx, s)
    kn = kn_ref[...]
    vn = vn_ref[...]
    sn = _dot_nt(wq, kn.astype(BF16)) + biasn_ref[...]
    m = jnp.maximum(jnp.max(mx, axis=1, keepdims=True), jnp.max(sn, axis=1, keepdims=True))

    pn = jnp.exp(sn - m)
    acc = _dot(pn.astype(BF16), vn.astype(BF16))
    lsum = None
    for p in range(n_pages):
        pe = jnp.exp(s_ref[p] - m)
        lsum = pe if lsum is None else lsum + pe
        acc = acc + _dot(pe.astype(BF16), v_refs[p][...].astype(BF16))
    l = jnp.sum(lsum, axis=1, keepdims=True) + jnp.sum(pn, axis=1, keepdims=True)

    half = n_rows // 2
    o1 = acc[:half] / l[:half]
    o2 = acc[half:] / l[half:]
    gate = jnp.concatenate([ga_ref[:, h * LANES:(h + 1) * LANES] for h in range(DA_HEADS)], axis=0)
    out = _diff_finish(o1, o2, _lambda_full(lam_ref, lam_init), lam_init, sub_ref[...], gate)
    for h in range(DA_HEADS):
        o_ref[:, h * LANES:(h + 1) * LANES] = out[h * t_new:(h + 1) * t_new].astype(o_ref.dtype)


def _sample_biases(t_new, page_rows):
    n_rows = 2 * DA_HEADS * t_new
    row = np.arange(n_rows)[:, None]
    row_head, row_q = (row // t_new) % DA_HEADS, row % t_new
    col = np.arange(page_rows)[None, :]
    bias = np.where(row_head == col % DA_HEADS, 0.0, -np.inf).astype(np.float32)
    coln = np.arange(DA_HEADS * t_new)[None, :]
    biasn = np.where((row_head == coln % DA_HEADS) & (coln // DA_HEADS <= row_q), 0.0, -np.inf).astype(np.float32)
    return jnp.asarray(bias), jnp.asarray(biasn)


def _attn_sample(page_table_flat, q, ck, cv, k_new, v_new, ga, lam, sub_w, layer_j, n_seq, t_new, n_pages,
                 lam_init):
    page_rows = ck.shape[2]
    kern = functools.partial(_attn_sample_kernel, t_new=t_new, n_pages=n_pages, lam_init=lam_init)
    bias, biasn = _sample_biases(t_new, page_rows)
    seq_map = lambda n, pt: (n, 0)
    const2 = lambda n, pt: (0, 0)
    n_rows = 2 * DA_HEADS * t_new

    def page_spec(p):
        return pl.BlockSpec((None, None, page_rows, LANES), lambda n, pt: (pt[n * n_pages + p], layer_j, 0, 0))

    seq_spec = lambda: pl.BlockSpec((t_new, DA_WIDTH), seq_map)
    new_spec = lambda: pl.BlockSpec((None, None, t_new * DA_HEADS, LANES), lambda n, pt: (n, layer_j, 0, 0))
    grid_spec = pltpu.PrefetchScalarGridSpec(
        num_scalar_prefetch=1,
        grid=(n_seq,),
        in_specs=([seq_spec()] + [page_spec(p) for p in range(n_pages)] + [page_spec(p) for p in range(n_pages)]
                  + [new_spec(), new_spec(), seq_spec(), pl.BlockSpec(lam.shape, const2),
                     pl.BlockSpec(sub_w.shape, const2), pl.BlockSpec(bias.shape, const2),
                     pl.BlockSpec(biasn.shape, const2)]),
        out_specs=seq_spec(),
        scratch_shapes=[pltpu.VMEM((n_pages, n_rows, page_rows), F32)],
    )
    return pl.pallas_call(
        kern,
        grid_spec=grid_spec,
        out_shape=jax.ShapeDtypeStruct((n_seq * t_new, DA_WIDTH), F32),
        compiler_params=_cparams(("arbitrary",)),
        name="attn_sample",
    )(page_table_flat, q, *([ck] * n_pages), *([cv] * n_pages), k_new, v_new, ga, lam, sub_w, bias, biasn)


def _out_ln_kernel(a_ref, b_ref, x_ref, w_ref, g_ref, bb_ref, y_ref):
    half = a_ref.shape[1]
    acc = _dot(a_ref[...].astype(BF16), w_ref[0:half, :]) + _dot(b_ref[...].astype(BF16), w_ref[half:2 * half, :])
    y_ref[...] = _layer_norm_rows(DN_ALPHA * x_ref[...] + acc, g_ref[...], bb_ref[...])


def _out_ln(a, b, x2d, w_out, ln_g, ln_b, tm):
    total = x2d.shape[0]
    row = lambda i: (i, 0)
    const = lambda i: (0, 0)
    return pl.pallas_call(
        _out_ln_kernel,
        grid=(total // tm,),
        in_specs=[
            pl.BlockSpec((tm, a.shape[1]), row),
            pl.BlockSpec((tm, b.shape[1]), row),
            pl.BlockSpec((tm, D_MODEL), row),
            pl.BlockSpec(w_out.shape, const),
            pl.BlockSpec(ln_g.shape, const),
            pl.BlockSpec(ln_b.shape, const),
        ],
        out_specs=pl.BlockSpec((tm, D_MODEL), row),
        out_shape=jax.ShapeDtypeStruct((total, D_MODEL), F32),
        compiler_params=_cparams(("arbitrary",)),
        name="out_ln",
    )(a, b, x2d, w_out, ln_g, ln_b)


def _gla_constants(top):
    big = GLA_BLOCK
    idx = np.arange(big)
    levels = int(round(math.log2(top)))
    seg = [((idx[:, None] // top == idx[None, :] // top) & (idx[None, :] <= idx[:, None])).astype(np.float32)]
    pairs = []
    n = top
    for _ in range(levels):
        start = (idx // n) * n
        rho = start + n // 2 - 1
        upper = idx > rho
        t = idx[None, :]
        c = np.where(upper[:, None], (t > rho[:, None]) & (t <= idx[:, None]), (t > idx[:, None]) & (t <= rho[:, None]))
        seg.append(c.astype(np.float32))
        pairs.append(((start[:, None] == start[None, :]) & upper[:, None] & ~upper[None, :]).astype(np.float32))
        n //= 2
    seg = np.concatenate(seg, axis=0)
    return jnp.asarray(np.concatenate([seg, seg], axis=1), BF16), jnp.asarray(np.stack(pairs)), levels


def _gla_intra(q_h, k_h, seg_h, pair_ref, levels):
    big = GLA_BLOCK
    att = jnp.zeros((big, big), F32)
    for lv in range(levels):
        e = jnp.exp2(seg_h[(1 + lv) * big:(2 + lv) * big, :])
        att = att + _dot_nt((q_h * e).astype(BF16), (k_h * e).astype(BF16)) * pair_ref[lv]
    diag = jnp.sum(q_h * k_h, axis=1, keepdims=True)
    row = lax.broadcasted_iota(jnp.int32, (big, big), 0)
    col = lax.broadcasted_iota(jnp.int32, (big, big), 1)
    return att + jnp.where(row == col, diag, 0.0)


def _odd_kernel(*refs, tn, tr, levels, has_state, n_alias):
    n_in = 11 + (1 if has_state else 0)
    refs = refs[:n_in] + refs[n_in + n_alias:]
    if has_state:
        (x_ref, w_ref, wga_ref, wgb_ref, bg_ref, nw_ref, wo_ref, lng_ref, lnb_ref, seg_ref, pair_ref,
         s0_ref, y_ref, sout_ref, q_s, k_s, v_s, g_s, o_s, st_s) = refs
    else:
        (x_ref, w_ref, wga_ref, wgb_ref, bg_ref, nw_ref, wo_ref, lng_ref, lnb_ref, seg_ref, pair_ref,
         y_ref, sout_ref, q_s, k_s, v_s, g_s, o_s, st_s) = refs
        s0_ref = None
    r = pl.program_id(1)
    rows = tn * tr
    big = GLA_BLOCK
    kw = GLA_HEADS * GLA_K_DIM
    vw = GLA_HEADS * GLA_V_DIM
    x = x_ref[...]
    xb = x.astype(BF16)
    q_s[...] = _dot(xb, w_ref[:, 0:kw]) * (GLA_K_DIM ** -0.5)
    k_s[...] = _dot(xb, w_ref[:, kw:2 * kw])
    v_s[...] = _dot(xb, w_ref[:, 2 * kw:2 * kw + vw]).astype(BF16)
    low = _dot(xb, wga_ref[...])
    z = _dot(low.astype(BF16), wgb_ref[...]) + bg_ref[...]
    g_s[...] = (jnp.minimum(z, 0.0) - jnp.log1p(jnp.exp(-jnp.abs(z)))) * (math.log2(math.e) / GLA_TAU)

    if not has_state:
        @pl.when(r == 0)
        def _():
            st_s[...] = jnp.zeros(st_s.shape, F32)

    def block(c, carry):
        r0 = pl.multiple_of(c * big, big)
        qb = q_s[pl.ds(r0, big), :]
        kb = k_s[pl.ds(r0, big), :]
        gb = g_s[pl.ds(r0, big), :]
        hi = gb.astype(BF16)
        lo = (gb - hi.astype(F32)).astype(BF16)
        seg = _dot(seg_ref[...], jnp.concatenate([hi, lo], axis=0))
        for h in range(GLA_HEADS):
            ks = slice(h * GLA_K_DIM, (h + 1) * GLA_K_DIM)
            q_h, k_h, seg_h = qb[:, ks], kb[:, ks], seg[:, ks]
            v_h = v_s[pl.ds(r0, big), h * GLA_V_DIM:(h + 1) * GLA_V_DIM]
            att = _gla_intra(q_h, k_h, seg_h, pair_ref, levels)
            o_h = _dot(att.astype(BF16), v_h)
            cb = seg_h[0:big, :]
            qe = q_h * jnp.exp2(cb)
            if has_state:
                n_runs = big // tr
                v_hf = v_h.astype(F32)
                pad_k = jnp.zeros((2 * SUBLANES - tr, GLA_K_DIM), F32)
                pad_v = jnp.zeros((2 * SUBLANES - tr, GLA_V_DIM), F32)
                inter = []
                for s in range(n_runs):
                    rs = slice(s * tr, (s + 1) * tr)
                    seq = c * n_runs + s
                    st = s0_ref[seq, h]
                    qep = jnp.concatenate([qe[rs], pad_k], axis=0).astype(BF16)
                    inter.append(_dot(qep, st.astype(BF16))[0:tr])
                    cb_last = cb[(s + 1) * tr - 1:(s + 1) * tr, :]
                    kt = k_h[rs] * jnp.exp2(cb_last - cb[rs])
                    ktp = jnp.concatenate([kt, pad_k], axis=0).astype(BF16)
                    vtp = jnp.concatenate([v_hf[rs], pad_v], axis=0).astype(BF16)
                    dcol = jnp.broadcast_to(jnp.exp2(cb_last), (GLA_K_DIM, GLA_K_DIM)).T
                    decay = jnp.concatenate([dcol] * (GLA_V_DIM // GLA_K_DIM), axis=1)
                    sout_ref[seq, h] = st * decay + _dot_tn(ktp, vtp)
                o_h = o_h + jnp.concatenate(inter, axis=0)
            else:
                st = st_s[h]
                o_h = o_h + _dot_nt(qe.astype(BF16), st.astype(BF16))
                cb_last = cb[big - 1:big, :]
                kt = (k_h * jnp.exp2(cb_last - cb)).astype(BF16)
                st_s[h] = st * jnp.exp2(cb_last) + _dot_tn(v_h, kt)
            o_s[pl.ds(r0, big), h * GLA_V_DIM:(h + 1) * GLA_V_DIM] = o_h
        return carry

    lax.fori_loop(0, rows // big, block, 0, unroll=4 if rows // big >= 4 else 1)

    if not has_state:
        @pl.when(r == pl.num_programs(1) - 1)
        def _():
            for h in range(GLA_HEADS):
                sout_ref[0, h] = st_s[h].T

    nw = nw_ref[...]
    for h in range(GLA_HEADS):
        vs = slice(h * GLA_V_DIM, (h + 1) * GLA_V_DIM)
        o_h = o_s[:, vs]
        ms = jnp.mean(o_h * o_h, axis=-1, keepdims=True)
        gate = _dot(xb, w_ref[:, 2 * kw + vw + h * GLA_V_DIM:2 * kw + vw + (h + 1) * GLA_V_DIM])
        o_s[:, vs] = o_h * lax.rsqrt(ms + RMS_EPS) * nw * _silu(gate)
    acc = _dot(o_s[...].astype(BF16), wo_ref[...])
    y_ref[...] = _layer_norm_rows(DN_ALPHA * x + acc, lng_ref[...], lnb_ref[...])


def _odd_layer(x2d, n_seq, t_seq, tn, tr, w_in, wga, wgb, bg, nw, wo, lng, lnb, state0, layer_j, n_odd,
               state_prev):
    has_state = state0 is not None
    seg, pair, levels = _gla_constants(tr if has_state else GLA_BLOCK)
    n_row_tiles = t_seq // tr
    rows = tn * tr
    total = n_seq * t_seq
    n_alias = 0 if state_prev is None else 1
    kern = functools.partial(_odd_kernel, tn=tn, tr=tr, levels=levels, has_state=has_state, n_alias=n_alias)
    row_map = lambda n, r: (n * n_row_tiles + r, 0)
    c2 = lambda n, r: (0, 0)
    c3 = lambda n, r: (0, 0, 0)
    st_map = lambda n, r: (n, layer_j, 0, 0, 0)
    in_specs = [
        pl.BlockSpec((rows, D_MODEL), row_map),
        pl.BlockSpec(w_in.shape, c2), pl.BlockSpec(wga.shape, c2), pl.BlockSpec(wgb.shape, c2),
        pl.BlockSpec(bg.shape, c2), pl.BlockSpec(nw.shape, c2), pl.BlockSpec(wo.shape, c2),
        pl.BlockSpec(lng.shape, c2), pl.BlockSpec(lnb.shape, c2),
        pl.BlockSpec(seg.shape, c2), pl.BlockSpec(pair.shape, c3),
    ]
    args = [x2d, w_in, wga, wgb, bg, nw, wo, lng, lnb, seg, pair]
    st_block = (tn, None, GLA_HEADS, GLA_K_DIM, GLA_V_DIM)
    if has_state:
        in_specs.append(pl.BlockSpec(st_block, st_map))
        args.append(state0)
    aliases = {}
    if state_prev is not None:
        aliases = {len(args): 1}
        in_specs.append(pl.BlockSpec(memory_space=pl.ANY))
        args.append(state_prev)
    kw = GLA_HEADS * GLA_K_DIM
    vw = GLA_HEADS * GLA_V_DIM
    return pl.pallas_call(
        kern,
        grid=(n_seq // tn, n_row_tiles),
        in_specs=in_specs,
        out_specs=[pl.BlockSpec((rows, D_MODEL), row_map), pl.BlockSpec(st_block, st_map)],
        out_shape=[jax.ShapeDtypeStruct((total, D_MODEL), F32),
                   jax.ShapeDtypeStruct((n_seq, n_odd, GLA_HEADS, GLA_K_DIM, GLA_V_DIM), F32)],
        input_output_aliases=aliases,
        scratch_shapes=[pltpu.VMEM((rows, kw), F32), pltpu.VMEM((rows, kw), F32), pltpu.VMEM((rows, vw), BF16),
                        pltpu.VMEM((rows, kw), F32), pltpu.VMEM((rows, vw), F32),
                        pltpu.VMEM((GLA_HEADS, GLA_V_DIM, GLA_K_DIM), F32)],
        compiler_params=_cparams(("arbitrary", "arbitrary")),
        name="odd_layer",
    )(*args)


def _rope_tables(positions):
    half = ROT_DIM // 2
    inv_freq = jnp.power(jnp.float32(ROPE_THETA), -jnp.arange(0, ROT_DIM, 2, dtype=jnp.float32) / ROT_DIM)
    ang = positions.astype(jnp.float32)[:, None] * inv_freq[None, :]
    cos, sin = jnp.cos(ang), jnp.sin(ang)
    n = positions.shape[0]
    ones = jnp.ones((n, DA_QK_DIM - ROT_DIM), F32)
    zeros = jnp.zeros((n, DA_QK_DIM - ROT_DIM), F32)
    zh = jnp.zeros((n, half), F32)
    cos_map = jnp.concatenate([cos, cos, ones], axis=1)
    sina_map = jnp.concatenate([zh, sin, zeros], axis=1)
    sinb_map = jnp.concatenate([-sin, zh, zeros], axis=1)
    two = lambda t: jnp.concatenate([t, t], axis=1)
    return two(cos_map), two(sina_map), two(sinb_map)


def kernel(x_prompt, x_sample, cache_k, cache_v, state_pool, state_gla, page_table, w_in_even, w_pool_lin,
           pool_scale, diff_lambda_params, subln_w, w_out_even, w_in_odd, w_gate_a, w_gate_b, b_gate, gla_norm_w,
           w_out_odd, ln_g, ln_b):
    n_p, t_p, _ = x_prompt.shape
    n_s, t_s, _ = x_sample.shape
    n_pool, n_even, page_size = cache_k.shape[:3]
    n_pages = page_table.shape[1]
    past_len = n_pages * page_size

    tr_p = min(512, t_p)
    tq_p = min(512, t_p)
    tn_even_s = 16
    tn_odd_s = GLA_BLOCK // t_s

    w_in_even_b = w_in_even.astype(BF16)
    w_pool_b = w_pool_lin.astype(BF16)
    w_out_even_b = w_out_even.astype(BF16)
    w_in_odd_b = w_in_odd.astype(BF16)
    w_out_odd_b = w_out_odd.astype(BF16)
    pad_rank = LANES - GLA_GATE_RANK
    w_ga_b = jnp.pad(w_gate_a, ((0, 0), (0, 0), (0, pad_rank))).astype(BF16)
    w_gb_b = jnp.pad(w_gate_b, ((0, 0), (0, pad_rank), (0, 0))).astype(BF16)

    cos_p, sina_p, sinb_p = _rope_tables(jnp.arange(t_p))
    cos_s, sina_s, sinb_s = (jnp.tile(t, (tn_even_s, 1)) for t in _rope_tables(past_len + jnp.arange(t_s)))

    ck = cache_k.reshape(n_pool, n_even, page_size * DA_HEADS, 2 * DA_QK_DIM)
    cv = cache_v.reshape(n_pool, n_even, page_size * DA_HEADS, DA_V_DIM)
    pt_flat = page_table.reshape(-1).astype(jnp.int32)
    pool16 = jnp.pad(state_pool, ((0, 0), (0, 0), (POOL_HIST - state_pool.shape[2], 0), (0, 0)))

    yp = x_prompt.reshape(n_p * t_p, D_MODEL)
    ys = x_sample.reshape(n_s * t_s, D_MODEL)
    n_odd = state_gla.shape[1]
    pp, psm = [], []
    kv_p = tuple(jnp.zeros((n_p, n_even, t_p * DA_HEADS, LANES), F32) for _ in range(2))
    kv_s = tuple(jnp.zeros((n_s, n_even, t_s * DA_HEADS, LANES), F32) for _ in range(2))
    gla_p = jnp.zeros((n_p, n_odd) + state_gla.shape[2:], F32)
    gla_s = jnp.zeros((n_s, n_odd) + state_gla.shape[2:], F32)
    for i in range(DEPTH):
        j = i // 2
        lng, lnb = ln_g[i][None, :], ln_b[i][None, :]
        if i % 2 == 0:
            lam_init = 0.8 - 0.6 * math.exp(-0.3 * i)
            pscale = pool_scale[j][None, :]
            sub_w = subln_w[j][None, :]
            lam = diff_lambda_params[j]
            w_vt = w_in_even_b[j][:, 2 * POOL_WIDTH + 2 * DA_WIDTH:2 * POOL_WIDTH + 3 * DA_WIDTH].T
            pmix, q, k32, v32, ga, npool, kbf, vt = _even_in(
                yp, n_p, t_p, 1, tr_p, tq_p, 0, w_in_even_b[j], w_pool_b[j], pscale, cos_p, sina_p, sinb_p, w_vt,
                None, j, n_even, kv_p, BF16)
            kv_p = (k32, v32)
            omix = _attn_flat(q, kbf, vt, ga, lam, sub_w, n_p, t_p, tq_p, lam_init)
            yp = _out_ln(pmix, omix, yp, w_out_even_b[j], lng, lnb, tr_p)
            pp.append(npool[:, 1:])
            pmix, q, k32, v32, ga, npool = _even_in(
                ys, n_s, t_s, tn_even_s, t_s, t_s, past_len, w_in_even_b[j], w_pool_b[j], pscale, cos_s, sina_s,
                sinb_s, None, pool16, j, n_even, kv_s, F32)
            kv_s = (k32, v32)
            omix = _attn_sample(pt_flat, q, ck, cv, k32, v32, ga, lam, sub_w, j, n_s, t_s, n_pages, lam_init)
            ys = _out_ln(pmix, omix, ys, w_out_even_b[j], lng, lnb, min(512, n_s * t_s))
            psm.append(npool[:, 1:])
        else:
            ow = (w_in_odd_b[j], w_ga_b[j], w_gb_b[j], b_gate[j][None, :], gla_norm_w[j][None, :], w_out_odd_b[j],
                  lng, lnb)
            yp, gla_p = _odd_layer(yp, n_p, t_p, 1, tr_p, *ow, None, j, n_odd, gla_p)
            ys, gla_s = _odd_layer(ys, n_s, t_s, tn_odd_s, t_s, *ow, state_gla, j, n_odd, gla_s)
    kv5 = lambda a, n, t: a.reshape(n, n_even, t, DA_HEADS, LANES)
    return (yp.reshape(n_p, t_p, D_MODEL), ys.reshape(n_s, t_s, D_MODEL),
            kv5(kv_p[0], n_p, t_p), kv5(kv_p[1], n_p, t_p), jnp.stack(pp, axis=1), gla_p,
            kv5(kv_s[0], n_s, t_s), kv5(kv_s[1], n_s, t_s), jnp.stack(psm, axis=1), gla_s)
```

```python
import functools
import math

import numpy as np
import jax
import jax.numpy as jnp
from jax import lax
from jax.experimental import pallas as pl
from jax.experimental.pallas import tpu as pltpu

F32 = jnp.float32
BF16 = jnp.bfloat16

LANES = 128
SUBLANES = 8
VMEM_LIMIT_BYTES = 56 * 1024 * 1024

D_MODEL = 1024
DEPTH = 4
POOL_WIDTH = 512
POOL_WINDOWS = (2, 4, 8, 16)
POOL_GROUP_DIM = 128
POOL_HIST = 16
DA_HEADS = 4
DA_QK_DIM = 64
DA_V_DIM = 128
DA_WIDTH = 512
ROT_DIM = 16
ROPE_THETA = 500000.0
GLA_HEADS = 4
GLA_K_DIM = 128
GLA_V_DIM = 256
GLA_GATE_RANK = 16
GLA_TAU = 16.0
GLA_BLOCK = 64
DN_ALPHA = (2 * DEPTH) ** 0.25
LN_EPS = 1e-5
RMS_EPS = 1e-5

_NT = (((1,), (1,)), ((), ()))
_TN = (((0,), (0,)), ((), ()))


def _dot(a, b):
    return jnp.dot(a, b, preferred_element_type=F32)


def _dot_nt(a, b):
    return lax.dot_general(a, b, _NT, preferred_element_type=F32)


def _dot_tn(a, b):
    return lax.dot_general(a, b, _TN, preferred_element_type=F32)


def _silu(x):
    return x * jax.nn.sigmoid(x)


def _cparams(sem):
    return pltpu.CompilerParams(dimension_semantics=sem, vmem_limit_bytes=VMEM_LIMIT_BYTES)


def _row_groups(rows, group=256):
    return [slice(r0, min(r0 + group, rows)) for r0 in range(0, rows, group)]


def _layer_norm_rows(z, g, b):
    mu = jnp.mean(z, axis=-1, keepdims=True)
    zc = z - mu
    var = jnp.mean(zc * zc, axis=-1, keepdims=True)
    return zc * lax.rsqrt(var + LN_EPS) * g + b


def _store_head_rows(out_ref, val, h, tn, tr):
    for s in range(tn):
        out_ref[s, pl.ds(h, tr, stride=DA_HEADS), :] = val[s * tr:(s + 1) * tr, :]


def _even_in_kernel(*refs, tn, tr, tk, pos0, has_past, n_alias):
    n_in = 7 + (1 if has_past else 0)
    refs = refs[:n_in] + refs[n_in + n_alias:]
    if has_past:
        (x_ref, w_ref, wlin_ref, pscale_ref, cos_ref, sina_ref, sinb_ref, buf_ref,
         pmix_ref, q_ref, k32_ref, v32_ref, ga_ref, npool_ref, ext_ref) = refs
        kbf_ref = vt_ref = None
    else:
        (x_ref, w_ref, wlin_ref, pscale_ref, cos_ref, sina_ref, sinb_ref,
         pmix_ref, q_ref, k32_ref, v32_ref, ga_ref, npool_ref, kbf_ref, vt_ref, ext_ref) = refs
        buf_ref = None
    r = pl.program_id(1)
    rows = tn * tr
    xb = x_ref[...].astype(BF16)

    def proj(c0, n):
        return _dot(xb, w_ref[:, c0:c0 + n])

    pu = proj(0, POOL_WIDTH)
    pg = proj(POOL_WIDTH, POOL_WIDTH)
    if has_past:
        ext_ref[:, 0:POOL_HIST, :] = buf_ref[...]
    else:
        @pl.when(r == 0)
        def _():
            ext_ref[:, 0:POOL_HIST, :] = jnp.zeros((tn, POOL_HIST, POOL_WIDTH), F32)

        @pl.when(r > 0)
        def _():
            ext_ref[:, 0:POOL_HIST, :] = ext_ref[:, tr:tr + POOL_HIST, :]
    for s in range(tn):
        ext_ref[s, POOL_HIST:POOL_HIST + tr, :] = pu[s * tr:(s + 1) * tr, :]
    npool_ref[...] = ext_ref[:, tr:tr + POOL_HIST, :]

    pos = pos0 + r * tr + lax.broadcasted_iota(jnp.int32, (tr, POOL_GROUP_DIM), 0)
    for g, w in enumerate(POOL_WINDOWS):
        c0 = g * POOL_GROUP_DIM
        cnt = jnp.minimum(pos + 1, w).astype(F32)
        pieces = []
        for s in range(tn):
            u_g = ext_ref[s, POOL_HIST:POOL_HIST + tr, c0:c0 + POOL_GROUP_DIM]
            acc = u_g
            for i in range(1, w):
                acc = acc + ext_ref[s, POOL_HIST - i:POOL_HIST - i + tr, c0:c0 + POOL_GROUP_DIM]
            pieces.append(acc / cnt - u_g)
        d = pieces[0] if tn == 1 else jnp.concatenate(pieces, axis=0)
        y = _dot(d.astype(BF16), wlin_ref[g]) * pscale_ref[:, c0:c0 + POOL_GROUP_DIM]
        pmix_ref[:, c0:c0 + POOL_GROUP_DIM] = (y * _silu(pg[:, c0:c0 + POOL_GROUP_DIM])).astype(pmix_ref.dtype)

    cos = cos_ref[...]
    sina = sina_ref[...]
    sinb = sinb_ref[...]

    def rope(xh):
        return (xh * cos + pltpu.roll(xh, ROT_DIM // 2, 1) * sina
                + pltpu.roll(xh, LANES - ROT_DIM // 2, 1) * sinb)

    q = proj(2 * POOL_WIDTH, DA_WIDTH)
    k = proj(2 * POOL_WIDTH + DA_WIDTH, DA_WIDTH)
    for h in range(DA_HEADS):
        sl = slice(h * LANES, (h + 1) * LANES)
        q_scale = DA_QK_DIM ** -0.5 * (1.0 if has_past else math.log2(math.e))
        q_ref[:, sl] = (rope(q[:, sl]) * q_scale).astype(q_ref.dtype)
        kr = rope(k[:, sl])
        _store_head_rows(k32_ref, kr, h, tn, tr)
        if kbf_ref is not None:
            kbf_ref[:, sl] = kr.astype(kbf_ref.dtype)
    v = proj(2 * POOL_WIDTH + 2 * DA_WIDTH, DA_WIDTH)
    for h in range(DA_HEADS):
        _store_head_rows(v32_ref, v[:, h * LANES:(h + 1) * LANES], h, tn, tr)
    if vt_ref is not None:
        vt = v.T.astype(vt_ref.dtype)
        for h in range(DA_HEADS):
            for c in range(tr // tk):
                vt_ref[h, c] = vt[h * DA_V_DIM:(h + 1) * DA_V_DIM, c * tk:(c + 1) * tk]
    ga_ref[...] = _silu(proj(2 * POOL_WIDTH + 3 * DA_WIDTH, DA_WIDTH)).astype(ga_ref.dtype)


def _even_in(x2d, n_seq, t_seq, tn, tr, tk, pos0, w_in, wlin, pscale, cos, sina, sinb, buf16, layer_j,
             n_even, kv_prev, act_dtype):
    n_row_tiles = t_seq // tr
    n_seq_tiles = n_seq // tn
    rows = tn * tr
    total = n_seq * t_seq
    has_past = buf16 is not None
    n_alias = 0 if kv_prev is None else 2
    kern = functools.partial(_even_in_kernel, tn=tn, tr=tr, tk=tk, pos0=pos0, has_past=has_past, n_alias=n_alias)
    row_map = lambda n, r: (n * n_row_tiles + r, 0)
    const2 = lambda n, r: (0, 0)
    tab_map = (lambda n, r: (r, 0)) if not has_past else const2
    in_specs = [
        pl.BlockSpec((rows, D_MODEL), row_map),
        pl.BlockSpec(w_in.shape, const2),
        pl.BlockSpec(wlin.shape, lambda n, r: (0, 0, 0)),
        pl.BlockSpec(pscale.shape, const2),
        pl.BlockSpec((rows, LANES), tab_map),
        pl.BlockSpec((rows, LANES), tab_map),
        pl.BlockSpec((rows, LANES), tab_map),
    ]
    args = [x2d, w_in, wlin, pscale, cos, sina, sinb]
    blk = lambda: pl.BlockSpec((rows, DA_WIDTH), row_map)
    sds = lambda dt: jax.ShapeDtypeStruct((total, DA_WIDTH), dt)
    kv_blk = lambda: pl.BlockSpec((tn, None, tr * DA_HEADS, LANES), lambda n, r: (n, layer_j, r, 0))
    kv_sds = jax.ShapeDtypeStruct((n_seq, n_even, t_seq * DA_HEADS, LANES), F32)
    out_specs = [blk(), blk(), kv_blk(), kv_blk(), blk(),
                 pl.BlockSpec((tn, POOL_HIST, POOL_WIDTH), lambda n, r: (n, 0, 0))]
    out_shape = [sds(act_dtype), sds(act_dtype), kv_sds, kv_sds, sds(act_dtype),
                 jax.ShapeDtypeStruct((n_seq, POOL_HIST, POOL_WIDTH), F32)]
    if has_past:
        in_specs.append(pl.BlockSpec((tn, None, POOL_HIST, POOL_WIDTH), lambda n, r: (n, layer_j, 0, 0)))
        args.append(buf16)
    else:
        out_specs += [blk(), pl.BlockSpec((None, DA_HEADS, tr // tk, DA_V_DIM, tk), lambda n, r: (n, 0, r, 0, 0))]
        out_shape += [sds(BF16), jax.ShapeDtypeStruct((n_seq, DA_HEADS, t_seq // tk, DA_V_DIM, tk), BF16)]
    aliases = {}
    if kv_prev is not None:
        aliases = {len(args): 2, len(args) + 1: 3}
        in_specs += [pl.BlockSpec(memory_space=pl.ANY), pl.BlockSpec(memory_space=pl.ANY)]
        args += list(kv_prev)
    return pl.pallas_call(
        kern,
        grid=(n_seq_tiles, n_row_tiles),
        in_specs=in_specs,
        out_specs=out_specs,
        out_shape=out_shape,
        input_output_aliases=aliases,
        scratch_shapes=[pltpu.VMEM((tn, POOL_HIST + tr, POOL_WIDTH), F32)],
        compiler_params=_cparams(("arbitrary", "arbitrary")),
        name="even_in",
    )(*args)


def _lambda_full(lam_ref, lam_init):
    lam = lam_ref[...]
    a = jnp.sum(lam[0:1, :] * lam[1:2, :], axis=1, keepdims=True)
    b = jnp.sum(lam[2:3, :] * lam[3:4, :], axis=1, keepdims=True)
    return jnp.exp(a) - jnp.exp(b) + lam_init


def _diff_finish(o1, o2, lam_full, lam_init, sub_w, gate):
    o = o1 - lam_full * o2
    ms = jnp.mean(o * o, axis=-1, keepdims=True)
    o = o * lax.rsqrt(ms + RMS_EPS) * sub_w * (1.0 - lam_init)
    return o * gate


def _stack_maps(q):
    lane = lax.broadcasted_iota(jnp.int32, q.shape, 1)
    zero = jnp.zeros_like(q)
    return jnp.concatenate([jnp.where(lane < DA_QK_DIM, q, zero), jnp.where(lane >= DA_QK_DIM, q, zero)], axis=0)


def _attn_flat_kernel(q_ref, k_ref, vt_ref, ga_ref, lam_ref, sub_ref, o_ref, acc_ref, s_ref, qs_ref,
                      *, tq, nq, lam_init):
    steps = [(qi, kb) for qi in range(nq) for kb in [qi] + list(range(qi))]
    lam_full = _lambda_full(lam_ref, lam_init)

    def scores(n):
        qi, kb = steps[n]
        if kb == qi:
            qs_ref[qi % 2] = _stack_maps(q_ref[qi * tq:(qi + 1) * tq, :])
        s = _dot_nt(k_ref[kb * tq:(kb + 1) * tq, :], qs_ref[qi % 2])
        if kb == qi:
            key = lax.broadcasted_iota(jnp.int32, s.shape, 0)
            qry = lax.broadcasted_iota(jnp.int32, s.shape, 1)
            qry = jnp.where(qry >= tq, qry - tq, qry)
            s = jnp.where(key <= qry, s, -jnp.inf)
        s_ref[n % 2] = s
        return jnp.max(s, axis=0, keepdims=True)

    m = l = None
    mx = scores(0)
    for n, (qi, kb) in enumerate(steps):
        mx_next = scores(n + 1) if n + 1 < len(steps) else None
        slot = qi % 2
        if kb == qi:
            m = mx
            p = jnp.exp2(s_ref[n % 2] - m)
            l = jnp.sum(p, axis=0, keepdims=True)
            acc_ref[slot] = _dot(vt_ref[kb], p.astype(BF16))
        else:
            m_new = jnp.maximum(m, mx)
            alpha = jnp.exp2(m - m_new)
            p = jnp.exp2(s_ref[n % 2] - m_new)
            l = alpha * l + jnp.sum(p, axis=0, keepdims=True)
            acc_ref[slot] = alpha * acc_ref[slot] + _dot(vt_ref[kb], p.astype(BF16))
            m = m_new
        if n + 1 == len(steps) or steps[n + 1][0] != qi:
            rows = slice(qi * tq, (qi + 1) * tq)
            o_t = acc_ref[slot] / l
            out = _diff_finish(o_t[:, :tq].T, o_t[:, tq:].T, lam_full, lam_init, sub_ref[...],
                               ga_ref[rows, :].astype(F32))
            o_ref[rows, :] = out.astype(o_ref.dtype)
        mx = mx_next


def _attn_flat(q, kbf, vt, ga, lam, sub_w, n_seq, t_seq, tq, lam_init):
    nq = t_seq // tq
    kern = functools.partial(_attn_flat_kernel, tq=tq, nq=nq, lam_init=lam_init)
    seq_head = lambda b, h: (b, h)
    c2 = lambda b, h: (0, 0)
    return pl.pallas_call(
        kern,
        grid=(n_seq, DA_HEADS),
        in_specs=[
            pl.BlockSpec((t_seq, LANES), seq_head),
            pl.BlockSpec((t_seq, LANES), seq_head),
            pl.BlockSpec((None, None, nq, DA_V_DIM, tq), lambda b, h: (b, h, 0, 0, 0)),
            pl.BlockSpec((t_seq, LANES), seq_head),
            pl.BlockSpec(lam.shape, c2),
            pl.BlockSpec(sub_w.shape, c2),
        ],
        out_specs=pl.BlockSpec((t_seq, LANES), seq_head),
        out_shape=jax.ShapeDtypeStruct((n_seq * t_seq, DA_WIDTH), BF16),
        scratch_shapes=[pltpu.VMEM((2, DA_V_DIM, 2 * tq), F32), pltpu.VMEM((2, tq, 2 * tq), F32),
                        pltpu.VMEM((2, 2 * tq, LANES), BF16)],
        compiler_params=_cparams(("arbitrary", "arbitrary")),
        name="attn_prompt",
    )(q, kbf, vt, ga, lam, sub_w)


def _attn_sample_kernel(*refs, t_new, n_pages, lam_init):
    pt_ref, q_ref = refs[0], refs[1]
    k_refs = refs[2:2 + n_pages]
    v_refs = refs[2 + n_pages:2 + 2 * n_pages]
    (kn_ref, vn_ref, ga_ref, lam_ref, sub_ref, bias_ref, biasn_ref, o_ref, s_ref) = refs[2 + 2 * n_pages:]
    del pt_ref
    n_rows = 2 * DA_HEADS * t_new

    pieces = [None] * (2 * DA_HEADS)
    for h in range(DA_HEADS):
        st = _stack_maps(q_ref[:, h * LANES:(h + 1) * LANES])
        pieces[h] = st[:t_new]
        pieces[DA_HEADS + h] = st[t_new:]
    wq = jnp.concatenate(pieces, axis=0).astype(BF16)

    bias = bias_ref[...]
    mx = None
    for p in range(n_pages):
        s = _dot_nt(wq, k_refs[p][...].astype(BF16)) + bias
        s_ref[p] = s
        mx = s if mx is None else jnp.maximum(mx, s)
    kn = kn_ref[...]
    vn = vn_ref[...]
    sn = _dot_nt(wq, kn.astype(BF16)) + biasn_ref[...]
    m = jnp.maximum(jnp.max(mx, axis=1, keepdims=True), jnp.max(sn, axis=1, keepdims=True))

    pn = jnp.exp(sn - m)
    acc = _dot(pn.astype(BF16), vn.astype(BF16))
    lsum = None
    for p in range(n_pages):
        pe = jnp.exp(s_ref[p] - m)
        lsum = pe if lsum is None else lsum + pe
        acc = acc + _dot(pe.astype(BF16), v_refs[p][...].astype(BF16))
    l = jnp.sum(lsum, axis=1, keepdims=True) + jnp.sum(pn, axis=1, keepdims=True)

    half = n_rows // 2
    o1 = acc[:half] / l[:half]
    o2 = acc[half:] / l[half:]
    gate = jnp.concatenate([ga_ref[:, h * LANES:(h + 1) * LANES] for h in range(DA_HEADS)], axis=0)
    out = _diff_finish(o1, o2, _lambda_full(lam_ref, lam_init), lam_init, sub_ref[...], gate)
    for h in range(DA_HEADS):
        o_ref[:, h * LANES:(h + 1) * LANES] = out[h * t_new:(h + 1) * t_new].astype(o_ref.dtype)


def _sample_biases(t_new, page_rows):
    n_rows = 2 * DA_HEADS * t_new
    row = np.arange(n_rows)[:, None]
    row_head, row_q = (row // t_new) % DA_HEADS, row % t_new
    col = np.arange(page_rows)[None, :]
    bias = np.where(row_head == col % DA_HEADS, 0.0, -np.inf).astype(np.float32)
    coln = np.arange(DA_HEADS * t_new)[None, :]
    biasn = np.where((row_head == coln % DA_HEADS) & (coln // DA_HEADS <= row_q), 0.0, -np.inf).astype(np.float32)
    return jnp.asarray(bias), jnp.asarray(biasn)


def _attn_sample(page_table_flat, q, ck, cv, k_new, v_new, ga, lam, sub_w, layer_j, n_seq, t_new, n_pages,
                 lam_init):
    page_rows = ck.shape[2]
    kern = functools.partial(_attn_sample_kernel, t_new=t_new, n_pages=n_pages, lam_init=lam_init)
    bias, biasn = _sample_biases(t_new, page_rows)
    seq_map = lambda n, pt: (n, 0)
    const2 = lambda n, pt: (0, 0)
    n_rows = 2 * DA_HEADS * t_new

    def page_spec(p):
        return pl.BlockSpec((None, None, page_rows, LANES), lambda n, pt: (pt[n * n_pages + p], layer_j, 0, 0))

    seq_spec = lambda: pl.BlockSpec((t_new, DA_WIDTH), seq_map)
    new_spec = lambda: pl.BlockSpec((None, None, t_new * DA_HEADS, LANES), lambda n, pt: (n, layer_j, 0, 0))
    grid_spec = pltpu.PrefetchScalarGridSpec(
        num_scalar_prefetch=1,
        grid=(n_seq,),
        in_specs=([seq_spec()] + [page_spec(p) for p in range(n_pages)] + [page_spec(p) for p in range(n_pages)]
                  + [new_spec(), new_spec(), seq_spec(), pl.BlockSpec(lam.shape, const2),
                     pl.BlockSpec(sub_w.shape, const2), pl.BlockSpec(bias.shape, const2),
                     pl.BlockSpec(biasn.shape, const2)]),
        out_specs=seq_spec(),
        scratch_shapes=[pltpu.VMEM((n_pages, n_rows, page_rows), F32)],
    )
    return pl.pallas_call(
        kern,
        grid_spec=grid_spec,
        out_shape=jax.ShapeDtypeStruct((n_seq * t_new, DA_WIDTH), F32),
        compiler_params=_cparams(("arbitrary",)),
        name="attn_sample",
    )(page_table_flat, q, *([ck] * n_pages), *([cv] * n_pages), k_new, v_new, ga, lam, sub_w, bias, biasn)


def _out_ln_kernel(a_ref, b_ref, x_ref, w_ref, g_ref, bb_ref, y_ref):
    half = a_ref.shape[1]
    for rs in _row_groups(a_ref.shape[0]):
        acc = (_dot(a_ref[rs, :].astype(BF16), w_ref[0:half, :])
               + _dot(b_ref[rs, :].astype(BF16), w_ref[half:2 * half, :]))
        y_ref[rs, :] = _layer_norm_rows(DN_ALPHA * x_ref[rs, :] + acc, g_ref[...], bb_ref[...])


def _out_ln(a, b, x2d, w_out, ln_g, ln_b, tm):
    total = x2d.shape[0]
    row = lambda i: (i, 0)
    const = lambda i: (0, 0)
    return pl.pallas_call(
        _out_ln_kernel,
        grid=(total // tm,),
        in_specs=[
            pl.BlockSpec((tm, a.shape[1]), row),
            pl.BlockSpec((tm, b.shape[1]), row),
            pl.BlockSpec((tm, D_MODEL), row),
            pl.BlockSpec(w_out.shape, const),
            pl.BlockSpec(ln_g.shape, const),
            pl.BlockSpec(ln_b.shape, const),
        ],
        out_specs=pl.BlockSpec((tm, D_MODEL), row),
        out_shape=jax.ShapeDtypeStruct((total, D_MODEL), F32),
        compiler_params=_cparams(("arbitrary",)),
        name="out_ln",
    )(a, b, x2d, w_out, ln_g, ln_b)


def _gla_constants(top):
    big = GLA_BLOCK
    idx = np.arange(big)
    levels = int(round(math.log2(top)))
    seg = [((idx[:, None] // top == idx[None, :] // top) & (idx[None, :] <= idx[:, None])).astype(np.float32)]
    pairs = []
    n = top
    for _ in range(levels):
        start = (idx // n) * n
        rho = start + n // 2 - 1
        upper = idx > rho
        t = idx[None, :]
        c = np.where(upper[:, None], (t > rho[:, None]) & (t <= idx[:, None]), (t > idx[:, None]) & (t <= rho[:, None]))
        seg.append(c.astype(np.float32))
        pairs.append(((start[:, None] == start[None, :]) & upper[:, None] & ~upper[None, :]).astype(np.float32))
        n //= 2
    seg = np.concatenate(seg, axis=0)
    return jnp.asarray(np.concatenate([seg, seg], axis=1), BF16), jnp.asarray(np.stack(pairs)), levels


def _gla_intra(q_h, k_h, seg_h, pair_ref, levels):
    big = GLA_BLOCK
    att = jnp.zeros((big, big), F32)
    for lv in range(levels):
        e = jnp.exp2(seg_h[(1 + lv) * big:(2 + lv) * big, :])
        att = att + _dot_nt((q_h * e).astype(BF16), (k_h * e).astype(BF16)) * pair_ref[lv]
    diag = jnp.sum(q_h * k_h, axis=1, keepdims=True)
    row = lax.broadcasted_iota(jnp.int32, (big, big), 0)
    col = lax.broadcasted_iota(jnp.int32, (big, big), 1)
    return att + jnp.where(row == col, diag, 0.0)


def _odd_kernel(*refs, tn, tr, levels, has_state, n_alias):
    n_in = 11 + (1 if has_state else 0)
    refs = refs[:n_in] + refs[n_in + n_alias:]
    if has_state:
        (x_ref, w_ref, wga_ref, wgb_ref, bg_ref, nw_ref, wo_ref, lng_ref, lnb_ref, seg_ref, pair_ref,
         s0_ref, y_ref, sout_ref, q_s, k_s, v_s, g_s, o_s, st_s) = refs
    else:
        (x_ref, w_ref, wga_ref, wgb_ref, bg_ref, nw_ref, wo_ref, lng_ref, lnb_ref, seg_ref, pair_ref,
         y_ref, sout_ref, q_s, k_s, v_s, g_s, o_s, st_s) = refs
        s0_ref = None
    r = pl.program_id(1)
    rows = tn * tr
    big = GLA_BLOCK
    kw = GLA_HEADS * GLA_K_DIM
    vw = GLA_HEADS * GLA_V_DIM
    x = x_ref[...]
    xb = x.astype(BF16)
    q_s[...] = _dot(xb, w_ref[:, 0:kw]) * (GLA_K_DIM ** -0.5)
    k_s[...] = _dot(xb, w_ref[:, kw:2 * kw])
    v_s[...] = _dot(xb, w_ref[:, 2 * kw:2 * kw + vw]).astype(BF16)
    low = _dot(xb, wga_ref[...])
    z = _dot(low.astype(BF16), wgb_ref[...]) + bg_ref[...]
    g_s[...] = (jnp.minimum(z, 0.0) - jnp.log1p(jnp.exp(-jnp.abs(z)))) * (math.log2(math.e) / GLA_TAU)

    if not has_state:
        @pl.when(r == 0)
        def _():
            st_s[...] = jnp.zeros(st_s.shape, F32)

    def block(c, carry):
        r0 = pl.multiple_of(c * big, big)
        qb = q_s[pl.ds(r0, big), :]
        kb = k_s[pl.ds(r0, big), :]
        gb = g_s[pl.ds(r0, big), :]
        hi = gb.astype(BF16)
        lo = (gb - hi.astype(F32)).astype(BF16)
        seg = _dot(seg_ref[...], jnp.concatenate([hi, lo], axis=0))
        for h in range(GLA_HEADS):
            ks = slice(h * GLA_K_DIM, (h + 1) * GLA_K_DIM)
            q_h, k_h, seg_h = qb[:, ks], kb[:, ks], seg[:, ks]
            v_h = v_s[pl.ds(r0, big), h * GLA_V_DIM:(h + 1) * GLA_V_DIM]
            att = _gla_intra(q_h, k_h, seg_h, pair_ref, levels)
            o_h = _dot(att.astype(BF16), v_h)
            cb = seg_h[0:big, :]
            qe = q_h * jnp.exp2(cb)
            if has_state:
                n_runs = big // tr
                v_hf = v_h.astype(F32)
                pad_k = jnp.zeros((2 * SUBLANES - tr, GLA_K_DIM), F32)
                pad_v = jnp.zeros((2 * SUBLANES - tr, GLA_V_DIM), F32)
                inter = []
                for s in range(n_runs):
                    rs = slice(s * tr, (s + 1) * tr)
                    seq = c * n_runs + s
                    st = s0_ref[seq, h]
                    qep = jnp.concatenate([qe[rs], pad_k], axis=0).astype(BF16)
                    inter.append(_dot(qep, st.astype(BF16))[0:tr])
                    cb_last = cb[(s + 1) * tr - 1:(s + 1) * tr, :]
                    kt = k_h[rs] * jnp.exp2(cb_last - cb[rs])
                    ktp = jnp.concatenate([kt, pad_k], axis=0).astype(BF16)
                    vtp = jnp.concatenate([v_hf[rs], pad_v], axis=0).astype(BF16)
                    dcol = jnp.broadcast_to(jnp.exp2(cb_last), (GLA_K_DIM, GLA_K_DIM)).T
                    decay = jnp.concatenate([dcol] * (GLA_V_DIM // GLA_K_DIM), axis=1)
                    sout_ref[seq, h] = st * decay + _dot_tn(ktp, vtp)
                o_h = o_h + jnp.concatenate(inter, axis=0)
            else:
                st = st_s[h]
                o_h = o_h + _dot_nt(qe.astype(BF16), st.astype(BF16))
                cb_last = cb[big - 1:big, :]
                kt = (k_h * jnp.exp2(cb_last - cb)).astype(BF16)
                st_s[h] = st * jnp.exp2(cb_last) + _dot_tn(v_h, kt)
            o_s[pl.ds(r0, big), h * GLA_V_DIM:(h + 1) * GLA_V_DIM] = o_h
        return carry

    lax.fori_loop(0, rows // big, block, 0, unroll=4 if rows // big >= 4 else 1)

    if not has_state:
        @pl.when(r == pl.num_programs(1) - 1)
        def _():
            for h in range(GLA_HEADS):
                sout_ref[0, h] = st_s[h].T

    nw = nw_ref[...]
    for h in range(GLA_HEADS):
        vs = slice(h * GLA_V_DIM, (h + 1) * GLA_V_DIM)
        o_h = o_s[:, vs]
        ms = jnp.mean(o_h * o_h, axis=-1, keepdims=True)
        gate = _dot(xb, w_ref[:, 2 * kw + vw + h * GLA_V_DIM:2 * kw + vw + (h + 1) * GLA_V_DIM])
        o_s[:, vs] = o_h * lax.rsqrt(ms + RMS_EPS) * nw * _silu(gate)
    for rs in _row_groups(rows):
        acc = _dot(o_s[rs, :].astype(BF16), wo_ref[...])
        y_ref[rs, :] = _layer_norm_rows(DN_ALPHA * x_ref[rs, :] + acc, lng_ref[...], lnb_ref[...])


def _odd_layer(x2d, n_seq, t_seq, tn, tr, w_in, wga, wgb, bg, nw, wo, lng, lnb, state0, layer_j, n_odd,
               state_prev):
    has_state = state0 is not None
    seg, pair, levels = _gla_constants(tr if has_state else GLA_BLOCK)
    n_row_tiles = t_seq // tr
    rows = tn * tr
    total = n_seq * t_seq
    n_alias = 0 if state_prev is None else 1
    kern = functools.partial(_odd_kernel, tn=tn, tr=tr, levels=levels, has_state=has_state, n_alias=n_alias)
    row_map = lambda n, r: (n * n_row_tiles + r, 0)
    c2 = lambda n, r: (0, 0)
    c3 = lambda n, r: (0, 0, 0)
    st_map = lambda n, r: (n, layer_j, 0, 0, 0)
    in_specs = [
        pl.BlockSpec((rows, D_MODEL), row_map),
        pl.BlockSpec(w_in.shape, c2), pl.BlockSpec(wga.shape, c2), pl.BlockSpec(wgb.shape, c2),
        pl.BlockSpec(bg.shape, c2), pl.BlockSpec(nw.shape, c2), pl.BlockSpec(wo.shape, c2),
        pl.BlockSpec(lng.shape, c2), pl.BlockSpec(lnb.shape, c2),
        pl.BlockSpec(seg.shape, c2), pl.BlockSpec(pair.shape, c3),
    ]
    args = [x2d, w_in, wga, wgb, bg, nw, wo, lng, lnb, seg, pair]
    st_block = (tn, None, GLA_HEADS, GLA_K_DIM, GLA_V_DIM)
    if has_state:
        in_specs.append(pl.BlockSpec(st_block, st_map))
        args.append(state0)
    aliases = {}
    if state_prev is not None:
        aliases = {len(args): 1}
        in_specs.append(pl.BlockSpec(memory_space=pl.ANY))
        args.append(state_prev)
    kw = GLA_HEADS * GLA_K_DIM
    vw = GLA_HEADS * GLA_V_DIM
    return pl.pallas_call(
        kern,
        grid=(n_seq // tn, n_row_tiles),
        in_specs=in_specs,
        out_specs=[pl.BlockSpec((rows, D_MODEL), row_map), pl.BlockSpec(st_block, st_map)],
        out_shape=[jax.ShapeDtypeStruct((total, D_MODEL), F32),
                   jax.ShapeDtypeStruct((n_seq, n_odd, GLA_HEADS, GLA_K_DIM, GLA_V_DIM), F32)],
        input_output_aliases=aliases,
        scratch_shapes=[pltpu.VMEM((rows, kw), F32), pltpu.VMEM((rows, kw), F32), pltpu.VMEM((rows, vw), BF16),
                        pltpu.VMEM((rows, kw), F32), pltpu.VMEM((rows, vw), F32),
                        pltpu.VMEM((GLA_HEADS, GLA_V_DIM, GLA_K_DIM), F32)],
        compiler_params=_cparams(("arbitrary", "arbitrary")),
        name="odd_layer",
    )(*args)


def _rope_tables(positions):
    half = ROT_DIM // 2
    inv_freq = jnp.power(jnp.float32(ROPE_THETA), -jnp.arange(0, ROT_DIM, 2, dtype=jnp.float32) / ROT_DIM)
    ang = positions.astype(jnp.float32)[:, None] * inv_freq[None, :]
    cos, sin = jnp.cos(ang), jnp.sin(ang)
    n = positions.shape[0]
    ones = jnp.ones((n, DA_QK_DIM - ROT_DIM), F32)
    zeros = jnp.zeros((n, DA_QK_DIM - ROT_DIM), F32)
    zh = jnp.zeros((n, half), F32)
    cos_map = jnp.concatenate([cos, cos, ones], axis=1)
    sina_map = jnp.concatenate([zh, sin, zeros], axis=1)
    sinb_map = jnp.concatenate([-sin, zh, zeros], axis=1)
    two = lambda t: jnp.concatenate([t, t], axis=1)
    return two(cos_map), two(sina_map), two(sinb_map)


def kernel(x_prompt, x_sample, cache_k, cache_v, state_pool, state_gla, page_table, w_in_even, w_pool_lin,
           pool_scale, diff_lambda_params, subln_w, w_out_even, w_in_odd, w_gate_a, w_gate_b, b_gate, gla_norm_w,
           w_out_odd, ln_g, ln_b):
    n_p, t_p, _ = x_prompt.shape
    n_s, t_s, _ = x_sample.shape
    n_pool, n_even, page_size = cache_k.shape[:3]
    n_pages = page_table.shape[1]
    past_len = n_pages * page_size

    tr_p = min(512, t_p)
    tq_p = min(512, t_p)
    tn_even_s = 16
    tn_odd_s = GLA_BLOCK // t_s

    w_in_even_b = w_in_even.astype(BF16)
    w_pool_b = w_pool_lin.astype(BF16)
    w_out_even_b = w_out_even.astype(BF16)
    w_in_odd_b = w_in_odd.astype(BF16)
    w_out_odd_b = w_out_odd.astype(BF16)
    pad_rank = LANES - GLA_GATE_RANK
    w_ga_b = jnp.pad(w_gate_a, ((0, 0), (0, 0), (0, pad_rank))).astype(BF16)
    w_gb_b = jnp.pad(w_gate_b, ((0, 0), (0, pad_rank), (0, 0))).astype(BF16)

    cos_p, sina_p, sinb_p = _rope_tables(jnp.arange(t_p))
    cos_s, sina_s, sinb_s = (jnp.tile(t, (tn_even_s, 1)) for t in _rope_tables(past_len + jnp.arange(t_s)))

    ck = cache_k.reshape(n_pool, n_even, page_size * DA_HEADS, 2 * DA_QK_DIM)
    cv = cache_v.reshape(n_pool, n_even, page_size * DA_HEADS, DA_V_DIM)
    pt_flat = page_table.reshape(-1).astype(jnp.int32)
    pool16 = jnp.pad(state_pool, ((0, 0), (0, 0), (POOL_HIST - state_pool.shape[2], 0), (0, 0)))

    yp = x_prompt.reshape(n_p * t_p, D_MODEL)
    ys = x_sample.reshape(n_s * t_s, D_MODEL)
    n_odd = state_gla.shape[1]
    pp, psm = [], []
    kv_p = tuple(jnp.zeros((n_p, n_even, t_p * DA_HEADS, LANES), F32) for _ in range(2))
    kv_s = tuple(jnp.zeros((n_s, n_even, t_s * DA_HEADS, LANES), F32) for _ in range(2))
    gla_p = jnp.zeros((n_p, n_odd) + state_gla.shape[2:], F32)
    gla_s = jnp.zeros((n_s, n_odd) + state_gla.shape[2:], F32)
    for i in range(DEPTH):
        j = i // 2
        lng, lnb = ln_g[i][None, :], ln_b[i][None, :]
        if i % 2 == 0:
            lam_init = 0.8 - 0.6 * math.exp(-0.3 * i)
            pscale = pool_scale[j][None, :]
            sub_w = subln_w[j][None, :]
            lam = diff_lambda_params[j]
            pmix, q, k32, v32, ga, npool, kbf, vt = _even_in(
                yp, n_p, t_p, 1, tr_p, tq_p, 0, w_in_even_b[j], w_pool_b[j], pscale, cos_p, sina_p, sinb_p,
                None, j, n_even, kv_p, BF16)
            kv_p = (k32, v32)
            omix = _attn_flat(q, kbf, vt, ga, lam, sub_w, n_p, t_p, tq_p, lam_init)
            yp = _out_ln(pmix, omix, yp, w_out_even_b[j], lng, lnb, min(2 * tr_p, t_p))
            pp.append(npool[:, 1:])
            pmix, q, k32, v32, ga, npool = _even_in(
                ys, n_s, t_s, tn_even_s, t_s, t_s, past_len, w_in_even_b[j], w_pool_b[j], pscale, cos_s, sina_s,
                sinb_s, pool16, j, n_even, kv_s, F32)
            kv_s = (k32, v32)
            omix = _attn_sample(pt_flat, q, ck, cv, k32, v32, ga, lam, sub_w, j, n_s, t_s, n_pages, lam_init)
            ys = _out_ln(pmix, omix, ys, w_out_even_b[j], lng, lnb, min(512, n_s * t_s))
            psm.append(npool[:, 1:])
        else:
            ow = (w_in_odd_b[j], w_ga_b[j], w_gb_b[j], b_gate[j][None, :], gla_norm_w[j][None, :], w_out_odd_b[j],
                  lng, lnb)
            yp, gla_p = _odd_layer(yp, n_p, t_p, 1, tr_p, *ow, None, j, n_odd, gla_p)
            ys, gla_s = _odd_layer(ys, n_s, t_s, tn_odd_s, t_s, *ow, state_gla, j, n_odd, gla_s)
    kv5 = lambda a, n, t: a.reshape(n, n_even, t, DA_HEADS, LANES)
    return (yp.reshape(n_p, t_p, D_MODEL), ys.reshape(n_s, t_s, D_MODEL),
            kv5(kv_p[0], n_p, t_p), kv5(kv_p[1], n_p, t_p), jnp.stack(pp, axis=1), gla_p,
            kv5(kv_s[0], n_s, t_s), kv5(kv_s[1], n_s, t_s), jnp.stack(psm, axis=1), gla_s)
```

```python
import functools
import math

import numpy as np
import jax
import jax.numpy as jnp
from jax import lax
from jax.experimental import pallas as pl
from jax.experimental.pallas import tpu as pltpu

F32 = jnp.float32
BF16 = jnp.bfloat16

LANES = 128
SUBLANES = 8
VMEM_LIMIT_BYTES = 56 * 1024 * 1024

D_MODEL = 1024
DEPTH = 4
POOL_WIDTH = 512
POOL_WINDOWS = (2, 4, 8, 16)
POOL_GROUP_DIM = 128
POOL_HIST = 16
DA_HEADS = 4
DA_QK_DIM = 64
DA_V_DIM = 128
DA_WIDTH = 512
ROT_DIM = 16
ROPE_THETA = 500000.0
GLA_HEADS = 4
GLA_K_DIM = 128
GLA_V_DIM = 256
GLA_GATE_RANK = 16
GLA_TAU = 16.0
GLA_BLOCK = 64
DN_ALPHA = (2 * DEPTH) ** 0.25
LN_EPS = 1e-5
RMS_EPS = 1e-5

_NT = (((1,), (1,)), ((), ()))
_TN = (((0,), (0,)), ((), ()))


def _dot(a, b):
    return jnp.dot(a, b, preferred_element_type=F32)


def _dot_nt(a, b):
    return lax.dot_general(a, b, _NT, preferred_element_type=F32)


def _dot_tn(a, b):
    return lax.dot_general(a, b, _TN, preferred_element_type=F32)


def _silu(x):
    return x * jax.nn.sigmoid(x)


def _cparams(sem):
    return pltpu.CompilerParams(dimension_semantics=sem, vmem_limit_bytes=VMEM_LIMIT_BYTES)


def _row_groups(rows, group=256):
    return [slice(r0, min(r0 + group, rows)) for r0 in range(0, rows, group)]


def _layer_norm_rows(z, g, b):
    mu = jnp.mean(z, axis=-1, keepdims=True)
    zc = z - mu
    var = jnp.mean(zc * zc, axis=-1, keepdims=True)
    return zc * lax.rsqrt(var + LN_EPS) * g + b


def _own_layer(out_ref, layer_j, n_alias):
    if n_alias:
        return ()
    for jj in range(out_ref.shape[1]):
        if jj != layer_j:
            out_ref[:, jj] = jnp.zeros(out_ref.shape[:1] + out_ref.shape[2:], out_ref.dtype)
    return (layer_j,)


def _store_head_rows(out_ref, layer, val, h, tn, tr):
    for s in range(tn):
        out_ref[(s,) + layer + (pl.ds(h, tr, stride=DA_HEADS), slice(None))] = val[s * tr:(s + 1) * tr, :]


def _even_in_kernel(*refs, tn, tr, tk, pos0, has_past, layer_j, n_alias):
    n_in = 7 + (1 if has_past else 0)
    refs = refs[:n_in] + refs[n_in + n_alias:]
    if has_past:
        (x_ref, w_ref, wlin_ref, pscale_ref, cos_ref, sina_ref, sinb_ref, buf_ref,
         pmix_ref, q_ref, k32_ref, v32_ref, ga_ref, npool_ref, ext_ref) = refs
        kbf_ref = vt_ref = None
    else:
        (x_ref, w_ref, wlin_ref, pscale_ref, cos_ref, sina_ref, sinb_ref,
         pmix_ref, q_ref, k32_ref, v32_ref, ga_ref, npool_ref, kbf_ref, vt_ref, ext_ref) = refs
        buf_ref = None
    k_layer = _own_layer(k32_ref, layer_j, n_alias)
    v_layer = _own_layer(v32_ref, layer_j, n_alias)
    r = pl.program_id(1)
    rows = tn * tr
    xb = x_ref[...].astype(BF16)

    def proj(c0, n):
        return _dot(xb, w_ref[:, c0:c0 + n])

    pu = proj(0, POOL_WIDTH)
    pg = proj(POOL_WIDTH, POOL_WIDTH)
    if has_past:
        ext_ref[:, 0:POOL_HIST, :] = buf_ref[...]
    else:
        @pl.when(r == 0)
        def _():
            ext_ref[:, 0:POOL_HIST, :] = jnp.zeros((tn, POOL_HIST, POOL_WIDTH), F32)

        @pl.when(r > 0)
        def _():
            ext_ref[:, 0:POOL_HIST, :] = ext_ref[:, tr:tr + POOL_HIST, :]
    for s in range(tn):
        ext_ref[s, POOL_HIST:POOL_HIST + tr, :] = pu[s * tr:(s + 1) * tr, :]
    npool_ref[...] = ext_ref[:, tr:tr + POOL_HIST, :]

    pos = pos0 + r * tr + lax.broadcasted_iota(jnp.int32, (tr, POOL_GROUP_DIM), 0)
    for g, w in enumerate(POOL_WINDOWS):
        c0 = g * POOL_GROUP_DIM
        cnt = jnp.minimum(pos + 1, w).astype(F32)
        pieces = []
        for s in range(tn):
            u_g = ext_ref[s, POOL_HIST:POOL_HIST + tr, c0:c0 + POOL_GROUP_DIM]
            acc = u_g
            for i in range(1, w):
                acc = acc + ext_ref[s, POOL_HIST - i:POOL_HIST - i + tr, c0:c0 + POOL_GROUP_DIM]
            pieces.append(acc / cnt - u_g)
        d = pieces[0] if tn == 1 else jnp.concatenate(pieces, axis=0)
        y = _dot(d.astype(BF16), wlin_ref[g]) * pscale_ref[:, c0:c0 + POOL_GROUP_DIM]
        pmix_ref[:, c0:c0 + POOL_GROUP_DIM] = (y * _silu(pg[:, c0:c0 + POOL_GROUP_DIM])).astype(pmix_ref.dtype)

    cos = cos_ref[...]
    sina = sina_ref[...]
    sinb = sinb_ref[...]

    def rope(xh):
        return (xh * cos + pltpu.roll(xh, ROT_DIM // 2, 1) * sina
                + pltpu.roll(xh, LANES - ROT_DIM // 2, 1) * sinb)

    q = proj(2 * POOL_WIDTH, DA_WIDTH)
    k = proj(2 * POOL_WIDTH + DA_WIDTH, DA_WIDTH)
    for h in range(DA_HEADS):
        sl = slice(h * LANES, (h + 1) * LANES)
        q_scale = DA_QK_DIM ** -0.5 * (1.0 if has_past else math.log2(math.e))
        q_ref[:, sl] = (rope(q[:, sl]) * q_scale).astype(q_ref.dtype)
        kr = rope(k[:, sl])
        _store_head_rows(k32_ref, k_layer, kr, h, tn, tr)
        if kbf_ref is not None:
            kbf_ref[:, sl] = kr.astype(kbf_ref.dtype)
    v = proj(2 * POOL_WIDTH + 2 * DA_WIDTH, DA_WIDTH)
    for h in range(DA_HEADS):
        _store_head_rows(v32_ref, v_layer, v[:, h * LANES:(h + 1) * LANES], h, tn, tr)
    if vt_ref is not None:
        vt = v.T.astype(vt_ref.dtype)
        for h in range(DA_HEADS):
            for c in range(tr // tk):
                vt_ref[h, c] = vt[h * DA_V_DIM:(h + 1) * DA_V_DIM, c * tk:(c + 1) * tk]
    ga_ref[...] = _silu(proj(2 * POOL_WIDTH + 3 * DA_WIDTH, DA_WIDTH)).astype(ga_ref.dtype)


def _even_in(x2d, n_seq, t_seq, tn, tr, tk, pos0, w_in, wlin, pscale, cos, sina, sinb, buf16, layer_j,
             n_even, kv_prev, act_dtype):
    n_row_tiles = t_seq // tr
    n_seq_tiles = n_seq // tn
    rows = tn * tr
    total = n_seq * t_seq
    has_past = buf16 is not None
    n_alias = 0 if kv_prev is None else 2
    kern = functools.partial(_even_in_kernel, tn=tn, tr=tr, tk=tk, pos0=pos0, has_past=has_past,
                             layer_j=layer_j, n_alias=n_alias)
    row_map = lambda n, r: (n * n_row_tiles + r, 0)
    const2 = lambda n, r: (0, 0)
    tab_map = (lambda n, r: (r, 0)) if not has_past else const2
    in_specs = [
        pl.BlockSpec((rows, D_MODEL), row_map),
        pl.BlockSpec(w_in.shape, const2),
        pl.BlockSpec(wlin.shape, lambda n, r: (0, 0, 0)),
        pl.BlockSpec(pscale.shape, const2),
        pl.BlockSpec((rows, LANES), tab_map),
        pl.BlockSpec((rows, LANES), tab_map),
        pl.BlockSpec((rows, LANES), tab_map),
    ]
    args = [x2d, w_in, wlin, pscale, cos, sina, sinb]
    blk = lambda: pl.BlockSpec((rows, DA_WIDTH), row_map)
    sds = lambda dt: jax.ShapeDtypeStruct((total, DA_WIDTH), dt)
    if kv_prev is None:
        kv_blk = lambda: pl.BlockSpec((tn, n_even, tr * DA_HEADS, LANES), lambda n, r: (n, 0, r, 0))
    else:
        kv_blk = lambda: pl.BlockSpec((tn, None, tr * DA_HEADS, LANES), lambda n, r: (n, layer_j, r, 0))
    kv_sds = jax.ShapeDtypeStruct((n_seq, n_even, t_seq * DA_HEADS, LANES), F32)
    out_specs = [blk(), blk(), kv_blk(), kv_blk(), blk(),
                 pl.BlockSpec((tn, POOL_HIST, POOL_WIDTH), lambda n, r: (n, 0, 0))]
    out_shape = [sds(act_dtype), sds(act_dtype), kv_sds, kv_sds, sds(act_dtype),
                 jax.ShapeDtypeStruct((n_seq, POOL_HIST, POOL_WIDTH), F32)]
    if has_past:
        in_specs.append(pl.BlockSpec((tn, None, POOL_HIST, POOL_WIDTH), lambda n, r: (n, layer_j, 0, 0)))
        args.append(buf16)
    else:
        out_specs += [blk(), pl.BlockSpec((None, DA_HEADS, tr // tk, DA_V_DIM, tk), lambda n, r: (n, 0, r, 0, 0))]
        out_shape += [sds(BF16), jax.ShapeDtypeStruct((n_seq, DA_HEADS, t_seq // tk, DA_V_DIM, tk), BF16)]
    aliases = {}
    if kv_prev is not None:
        aliases = {len(args): 2, len(args) + 1: 3}
        in_specs += [pl.BlockSpec(memory_space=pl.ANY), pl.BlockSpec(memory_space=pl.ANY)]
        args += list(kv_prev)
    return pl.pallas_call(
        kern,
        grid=(n_seq_tiles, n_row_tiles),
        in_specs=in_specs,
        out_specs=out_specs,
        out_shape=out_shape,
        input_output_aliases=aliases,
        scratch_shapes=[pltpu.VMEM((tn, POOL_HIST + tr, POOL_WIDTH), F32)],
        compiler_params=_cparams(("arbitrary", "arbitrary")),
        name="even_in",
    )(*args)


def _lambda_full(lam_ref, lam_init):
    lam = lam_ref[...]
    a = jnp.sum(lam[0:1, :] * lam[1:2, :], axis=1, keepdims=True)
    b = jnp.sum(lam[2:3, :] * lam[3:4, :], axis=1, keepdims=True)
    return jnp.exp(a) - jnp.exp(b) + lam_init


def _diff_finish(o1, o2, lam_full, lam_init, sub_w, gate):
    o = o1 - lam_full * o2
    ms = jnp.mean(o * o, axis=-1, keepdims=True)
    o = o * lax.rsqrt(ms + RMS_EPS) * sub_w * (1.0 - lam_init)
    return o * gate


def _stack_maps(q):
    lane = lax.broadcasted_iota(jnp.int32, q.shape, 1)
    zero = jnp.zeros_like(q)
    return jnp.concatenate([jnp.where(lane < DA_QK_DIM, q, zero), jnp.where(lane >= DA_QK_DIM, q, zero)], axis=0)


def _attn_flat_kernel(q_ref, k_ref, vt_ref, ga_ref, lam_ref, sub_ref, o_ref, acc_ref, s_ref, qs_ref,
                      *, tq, nq, lam_init):
    steps = [(qi, kb) for qi in range(nq) for kb in [qi] + list(range(qi))]
    lam_full = _lambda_full(lam_ref, lam_init)

    def scores(n):
        qi, kb = steps[n]
        if kb == qi:
            qs_ref[qi % 2] = _stack_maps(q_ref[qi * tq:(qi + 1) * tq, :])
        s = _dot_nt(k_ref[kb * tq:(kb + 1) * tq, :], qs_ref[qi % 2])
        if kb == qi:
            key = lax.broadcasted_iota(jnp.int32, s.shape, 0)
            qry = lax.broadcasted_iota(jnp.int32, s.shape, 1)
            qry = jnp.where(qry >= tq, qry - tq, qry)
            s = jnp.where(key <= qry, s, -jnp.inf)
        s_ref[n % 2] = s
        return jnp.max(s, axis=0, keepdims=True)

    m = l = None
    mx = scores(0)
    for n, (qi, kb) in enumerate(steps):
        mx_next = scores(n + 1) if n + 1 < len(steps) else None
        slot = qi % 2
        if kb == qi:
            m = mx
            p = jnp.exp2(s_ref[n % 2] - m)
            l = jnp.sum(p, axis=0, keepdims=True)
            acc_ref[slot] = _dot(vt_ref[kb], p.astype(BF16))
        else:
            m_new = jnp.maximum(m, mx)
            alpha = jnp.exp2(m - m_new)
            p = jnp.exp2(s_ref[n % 2] - m_new)
            l = alpha * l + jnp.sum(p, axis=0, keepdims=True)
            acc_ref[slot] = alpha * acc_ref[slot] + _dot(vt_ref[kb], p.astype(BF16))
            m = m_new
        if n + 1 == len(steps) or steps[n + 1][0] != qi:
            rows = slice(qi * tq, (qi + 1) * tq)
            o_t = acc_ref[slot] / l
            out = _diff_finish(o_t[:, :tq].T, o_t[:, tq:].T, lam_full, lam_init, sub_ref[...],
                               ga_ref[rows, :].astype(F32))
            o_ref[rows, :] = out.astype(o_ref.dtype)
        mx = mx_next


def _attn_flat(q, kbf, vt, ga, lam, sub_w, n_seq, t_seq, tq, lam_init):
    nq = t_seq // tq
    kern = functools.partial(_attn_flat_kernel, tq=tq, nq=nq, lam_init=lam_init)
    seq_head = lambda b, h: (b, h)
    c2 = lambda b, h: (0, 0)
    return pl.pallas_call(
        kern,
        grid=(n_seq, DA_HEADS),
        in_specs=[
            pl.BlockSpec((t_seq, LANES), seq_head),
            pl.BlockSpec((t_seq, LANES), seq_head),
            pl.BlockSpec((None, None, nq, DA_V_DIM, tq), lambda b, h: (b, h, 0, 0, 0)),
            pl.BlockSpec((t_seq, LANES), seq_head),
            pl.BlockSpec(lam.shape, c2),
            pl.BlockSpec(sub_w.shape, c2),
        ],
        out_specs=pl.BlockSpec((t_seq, LANES), seq_head),
        out_shape=jax.ShapeDtypeStruct((n_seq * t_seq, DA_WIDTH), BF16),
        scratch_shapes=[pltpu.VMEM((2, DA_V_DIM, 2 * tq), F32), pltpu.VMEM((2, tq, 2 * tq), F32),
                        pltpu.VMEM((2, 2 * tq, LANES), BF16)],
        compiler_params=_cparams(("arbitrary", "arbitrary")),
        name="attn_prompt",
    )(q, kbf, vt, ga, lam, sub_w)


def _attn_sample_kernel(*refs, t_new, n_pages, lam_init):
    pt_ref, q_ref = refs[0], refs[1]
    k_refs = refs[2:2 + n_pages]
    v_refs = refs[2 + n_pages:2 + 2 * n_pages]
    (kn_ref, vn_ref, ga_ref, lam_ref, sub_ref, bias_ref, biasn_ref, o_ref, s_ref) = refs[2 + 2 * n_pages:]
    del pt_ref
    n_rows = 2 * DA_HEADS * t_new

    pieces = [None] * (2 * DA_HEADS)
    for h in range(DA_HEADS):
        st = _stack_maps(q_ref[:, h * LANES:(h + 1) * LANES])
        pieces[h] = st[:t_new]
        pieces[DA_HEADS + h] = st[t_new:]
    wq = jnp.concatenate(pieces, axis=0).astype(BF16)

    bias = bias_ref[...]
    mx = None
    for p in range(n_pages):
        s = _dot_nt(wq, k_refs[p][...].astype(BF16)) + bias
        s_ref[p] = s
        mx = s if mx is None else jnp.maximum(mx, s)
    kn = kn_ref[...]
    vn = vn_ref[...]
    sn = _dot_nt(wq, kn.astype(BF16)) + biasn_ref[...]
    m = jnp.maximum(jnp.max(mx, axis=1, keepdims=True), jnp.max(sn, axis=1, keepdims=True))

    pn = jnp.exp(sn - m)
    acc = _dot(pn.astype(BF16), vn.astype(BF16))
    lsum = None
    for p in range(n_pages):
        pe = jnp.exp(s_ref[p] - m)
        lsum = pe if lsum is None else lsum + pe
        acc = acc + _dot(pe.astype(BF16), v_refs[p][...].astype(BF16))
    l = jnp.sum(lsum, axis=1, keepdims=True) + jnp.sum(pn, axis=1, keepdims=True)

    half = n_rows // 2
    o1 = acc[:half] / l[:half]
    o2 = acc[half:] / l[half:]
    gate = jnp.concatenate([ga_ref[:, h * LANES:(h + 1) * LANES] for h in range(DA_HEADS)], axis=0)
    out = _diff_finish(o1, o2, _lambda_full(lam_ref, lam_init), lam_init, sub_ref[...], gate)
    for h in range(DA_HEADS):
        o_ref[:, h * LANES:(h + 1) * LANES] = out[h * t_new:(h + 1) * t_new].astype(o_ref.dtype)


def _sample_biases(t_new, page_rows):
    n_rows = 2 * DA_HEADS * t_new
    row = np.arange(n_rows)[:, None]
    row_head, row_q = (row // t_new) % DA_HEADS, row % t_new
    col = np.arange(page_rows)[None, :]
    bias = np.where(row_head == col % DA_HEADS, 0.0, -np.inf).astype(np.float32)
    coln = np.arange(DA_HEADS * t_new)[None, :]
    biasn = np.where((row_head == coln % DA_HEADS) & (coln // DA_HEADS <= row_q), 0.0, -np.inf).astype(np.float32)
    return jnp.asarray(bias), jnp.asarray(biasn)


def _attn_sample(page_table_flat, q, ck, cv, k_new, v_new, ga, lam, sub_w, layer_j, n_seq, t_new, n_pages,
                 lam_init):
    page_rows = ck.shape[2]
    kern = functools.partial(_attn_sample_kernel, t_new=t_new, n_pages=n_pages, lam_init=lam_init)
    bias, biasn = _sample_biases(t_new, page_rows)
    seq_map = lambda n, pt: (n, 0)
    const2 = lambda n, pt: (0, 0)
    n_rows = 2 * DA_HEADS * t_new

    def page_spec(p):
        return pl.BlockSpec((None, None, page_rows, LANES), lambda n, pt: (pt[n * n_pages + p], layer_j, 0, 0))

    seq_spec = lambda: pl.BlockSpec((t_new, DA_WIDTH), seq_map)
    new_spec = lambda: pl.BlockSpec((None, None, t_new * DA_HEADS, LANES), lambda n, pt: (n, layer_j, 0, 0))
    grid_spec = pltpu.PrefetchScalarGridSpec(
        num_scalar_prefetch=1,
        grid=(n_seq,),
        in_specs=([seq_spec()] + [page_spec(p) for p in range(n_pages)] + [page_spec(p) for p in range(n_pages)]
                  + [new_spec(), new_spec(), seq_spec(), pl.BlockSpec(lam.shape, const2),
                     pl.BlockSpec(sub_w.shape, const2), pl.BlockSpec(bias.shape, const2),
                     pl.BlockSpec(biasn.shape, const2)]),
        out_specs=seq_spec(),
        scratch_shapes=[pltpu.VMEM((n_pages, n_rows, page_rows), F32)],
    )
    return pl.pallas_call(
        kern,
        grid_spec=grid_spec,
        out_shape=jax.ShapeDtypeStruct((n_seq * t_new, DA_WIDTH), F32),
        compiler_params=_cparams(("arbitrary",)),
        name="attn_sample",
    )(page_table_flat, q, *([ck] * n_pages), *([cv] * n_pages), k_new, v_new, ga, lam, sub_w, bias, biasn)


def _out_ln_kernel(a_ref, b_ref, x_ref, w_ref, g_ref, bb_ref, y_ref):
    half = a_ref.shape[1]
    for rs in _row_groups(a_ref.shape[0]):
        acc = (_dot(a_ref[rs, :].astype(BF16), w_ref[0:half, :])
               + _dot(b_ref[rs, :].astype(BF16), w_ref[half:2 * half, :]))
        y_ref[rs, :] = _layer_norm_rows(DN_ALPHA * x_ref[rs, :] + acc, g_ref[...], bb_ref[...])


def _out_ln(a, b, x2d, w_out, ln_g, ln_b, tm):
    total = x2d.shape[0]
    row = lambda i: (i, 0)
    const = lambda i: (0, 0)
    return pl.pallas_call(
        _out_ln_kernel,
        grid=(total // tm,),
        in_specs=[
            pl.BlockSpec((tm, a.shape[1]), row),
            pl.BlockSpec((tm, b.shape[1]), row),
            pl.BlockSpec((tm, D_MODEL), row),
            pl.BlockSpec(w_out.shape, const),
            pl.BlockSpec(ln_g.shape, const),
            pl.BlockSpec(ln_b.shape, const),
        ],
        out_specs=pl.BlockSpec((tm, D_MODEL), row),
        out_shape=jax.ShapeDtypeStruct((total, D_MODEL), F32),
        compiler_params=_cparams(("arbitrary",)),
        name="out_ln",
    )(a, b, x2d, w_out, ln_g, ln_b)


def _gla_constants(top):
    big = GLA_BLOCK
    idx = np.arange(big)
    levels = int(round(math.log2(top)))
    seg = [((idx[:, None] // top == idx[None, :] // top) & (idx[None, :] <= idx[:, None])).astype(np.float32)]
    pairs = []
    n = top
    for _ in range(levels):
        start = (idx // n) * n
        rho = start + n // 2 - 1
        upper = idx > rho
        t = idx[None, :]
        c = np.where(upper[:, None], (t > rho[:, None]) & (t <= idx[:, None]), (t > idx[:, None]) & (t <= rho[:, None]))
        seg.append(c.astype(np.float32))
        pairs.append(((start[:, None] == start[None, :]) & upper[:, None] & ~upper[None, :]).astype(np.float32))
        n //= 2
    seg = np.concatenate(seg, axis=0)
    return jnp.asarray(np.concatenate([seg, seg], axis=1), BF16), jnp.asarray(np.stack(pairs)), levels


def _gla_intra(q_h, k_h, seg_h, pair_ref, levels):
    big = GLA_BLOCK
    att = jnp.zeros((big, big), F32)
    for lv in range(levels):
        e = jnp.exp2(seg_h[(1 + lv) * big:(2 + lv) * big, :])
        att = att + _dot_nt((q_h * e).astype(BF16), (k_h * e).astype(BF16)) * pair_ref[lv]
    diag = jnp.sum(q_h * k_h, axis=1, keepdims=True)
    row = lax.broadcasted_iota(jnp.int32, (big, big), 0)
    col = lax.broadcasted_iota(jnp.int32, (big, big), 1)
    return att + jnp.where(row == col, diag, 0.0)


def _odd_kernel(*refs, tn, tr, levels, has_state, layer_j, n_alias):
    n_in = 11 + (1 if has_state else 0)
    refs = refs[:n_in] + refs[n_in + n_alias:]
    if has_state:
        (x_ref, w_ref, wga_ref, wgb_ref, bg_ref, nw_ref, wo_ref, lng_ref, lnb_ref, seg_ref, pair_ref,
         s0_ref, y_ref, sout_ref, q_s, k_s, v_s, g_s, o_s, st_s) = refs
    else:
        (x_ref, w_ref, wga_ref, wgb_ref, bg_ref, nw_ref, wo_ref, lng_ref, lnb_ref, seg_ref, pair_ref,
         y_ref, sout_ref, q_s, k_s, v_s, g_s, o_s, st_s) = refs
        s0_ref = None
    s_layer = _own_layer(sout_ref, layer_j, n_alias)
    r = pl.program_id(1)
    rows = tn * tr
    big = GLA_BLOCK
    kw = GLA_HEADS * GLA_K_DIM
    vw = GLA_HEADS * GLA_V_DIM
    x = x_ref[...]
    xb = x.astype(BF16)
    q_s[...] = _dot(xb, w_ref[:, 0:kw]) * (GLA_K_DIM ** -0.5)
    k_s[...] = _dot(xb, w_ref[:, kw:2 * kw])
    v_s[...] = _dot(xb, w_ref[:, 2 * kw:2 * kw + vw]).astype(BF16)
    low = _dot(xb, wga_ref[...])
    z = _dot(low.astype(BF16), wgb_ref[...]) + bg_ref[...]
    g_s[...] = (jnp.minimum(z, 0.0) - jnp.log1p(jnp.exp(-jnp.abs(z)))) * (math.log2(math.e) / GLA_TAU)

    if not has_state:
        @pl.when(r == 0)
        def _():
            st_s[...] = jnp.zeros(st_s.shape, F32)

    def block(c, carry):
        r0 = pl.multiple_of(c * big, big)
        qb = q_s[pl.ds(r0, big), :]
        kb = k_s[pl.ds(r0, big), :]
        gb = g_s[pl.ds(r0, big), :]
        hi = gb.astype(BF16)
        lo = (gb - hi.astype(F32)).astype(BF16)
        seg = _dot(seg_ref[...], jnp.concatenate([hi, lo], axis=0))
        for h in range(GLA_HEADS):
            ks = slice(h * GLA_K_DIM, (h + 1) * GLA_K_DIM)
            q_h, k_h, seg_h = qb[:, ks], kb[:, ks], seg[:, ks]
            v_h = v_s[pl.ds(r0, big), h * GLA_V_DIM:(h + 1) * GLA_V_DIM]
            att = _gla_intra(q_h, k_h, seg_h, pair_ref, levels)
            o_h = _dot(att.astype(BF16), v_h)
            cb = seg_h[0:big, :]
            qe = q_h * jnp.exp2(cb)
            if has_state:
                n_runs = big // tr
                v_hf = v_h.astype(F32)
                pad_k = jnp.zeros((2 * SUBLANES - tr, GLA_K_DIM), F32)
                pad_v = jnp.zeros((2 * SUBLANES - tr, GLA_V_DIM), F32)
                inter = []
                for s in range(n_runs):
                    rs = slice(s * tr, (s + 1) * tr)
                    seq = c * n_runs + s
                    st = s0_ref[seq, h]
                    qep = jnp.concatenate([qe[rs], pad_k], axis=0).astype(BF16)
                    inter.append(_dot(qep, st.astype(BF16))[0:tr])
                    cb_last = cb[(s + 1) * tr - 1:(s + 1) * tr, :]
                    kt = k_h[rs] * jnp.exp2(cb_last - cb[rs])
                    ktp = jnp.concatenate([kt, pad_k], axis=0).astype(BF16)
                    vtp = jnp.concatenate([v_hf[rs], pad_v], axis=0).astype(BF16)
                    dcol = jnp.broadcast_to(jnp.exp2(cb_last), (GLA_K_DIM, GLA_K_DIM)).T
                    decay = jnp.concatenate([dcol] * (GLA_V_DIM // GLA_K_DIM), axis=1)
                    sout_ref[(seq,) + s_layer + (h,)] = st * decay + _dot_tn(ktp, vtp)
                o_h = o_h + jnp.concatenate(inter, axis=0)
            else:
                st = st_s[h]
                o_h = o_h + _dot_nt(qe.astype(BF16), st.astype(BF16))
                cb_last = cb[big - 1:big, :]
                kt = (k_h * jnp.exp2(cb_last - cb)).astype(BF16)
                st_s[h] = st * jnp.exp2(cb_last) + _dot_tn(v_h, kt)
            o_s[pl.ds(r0, big), h * GLA_V_DIM:(h + 1) * GLA_V_DIM] = o_h
        return carry

    lax.fori_loop(0, rows // big, block, 0, unroll=4 if rows // big >= 4 else 1)

    if not has_state:
        @pl.when(r == pl.num_programs(1) - 1)
        def _():
            for h in range(GLA_HEADS):
                sout_ref[(0,) + s_layer + (h,)] = st_s[h].T

    nw = nw_ref[...]
    for h in range(GLA_HEADS):
        vs = slice(h * GLA_V_DIM, (h + 1) * GLA_V_DIM)
        o_h = o_s[:, vs]
        ms = jnp.mean(o_h * o_h, axis=-1, keepdims=True)
        gate = _dot(xb, w_ref[:, 2 * kw + vw + h * GLA_V_DIM:2 * kw + vw + (h + 1) * GLA_V_DIM])
        o_s[:, vs] = o_h * lax.rsqrt(ms + RMS_EPS) * nw * _silu(gate)
    for rs in _row_groups(rows):
        acc = _dot(o_s[rs, :].astype(BF16), wo_ref[...])
        y_ref[rs, :] = _layer_norm_rows(DN_ALPHA * x_ref[rs, :] + acc, lng_ref[...], lnb_ref[...])


def _odd_layer(x2d, n_seq, t_seq, tn, tr, w_in, wga, wgb, bg, nw, wo, lng, lnb, state0, layer_j, n_odd,
               state_prev):
    has_state = state0 is not None
    seg, pair, levels = _gla_constants(tr if has_state else GLA_BLOCK)
    n_row_tiles = t_seq // tr
    rows = tn * tr
    total = n_seq * t_seq
    n_alias = 0 if state_prev is None else 1
    kern = functools.partial(_odd_kernel, tn=tn, tr=tr, levels=levels, has_state=has_state, layer_j=layer_j,
                             n_alias=n_alias)
    row_map = lambda n, r: (n * n_row_tiles + r, 0)
    c2 = lambda n, r: (0, 0)
    c3 = lambda n, r: (0, 0, 0)
    st_map = lambda n, r: (n, layer_j, 0, 0, 0)
    in_specs = [
        pl.BlockSpec((rows, D_MODEL), row_map),
        pl.BlockSpec(w_in.shape, c2), pl.BlockSpec(wga.shape, c2), pl.BlockSpec(wgb.shape, c2),
        pl.BlockSpec(bg.shape, c2), pl.BlockSpec(nw.shape, c2), pl.BlockSpec(wo.shape, c2),
        pl.BlockSpec(lng.shape, c2), pl.BlockSpec(lnb.shape, c2),
        pl.BlockSpec(seg.shape, c2), pl.BlockSpec(pair.shape, c3),
    ]
    args = [x2d, w_in, wga, wgb, bg, nw, wo, lng, lnb, seg, pair]
    st_block = (tn, None, GLA_HEADS, GLA_K_DIM, GLA_V_DIM)
    if has_state:
        in_specs.append(pl.BlockSpec(st_block, st_map))
        args.append(state0)
    aliases = {}
    if state_prev is None:
        out_state = pl.BlockSpec((tn, n_odd, GLA_HEADS, GLA_K_DIM, GLA_V_DIM), lambda n, r: (n, 0, 0, 0, 0))
    else:
        out_state = pl.BlockSpec(st_block, st_map)
        aliases = {len(args): 1}
        in_specs.append(pl.BlockSpec(memory_space=pl.ANY))
        args.append(state_prev)
    kw = GLA_HEADS * GLA_K_DIM
    vw = GLA_HEADS * GLA_V_DIM
    return pl.pallas_call(
        kern,
        grid=(n_seq // tn, n_row_tiles),
        in_specs=in_specs,
        out_specs=[pl.BlockSpec((rows, D_MODEL), row_map), out_state],
        out_shape=[jax.ShapeDtypeStruct((total, D_MODEL), F32),
                   jax.ShapeDtypeStruct((n_seq, n_odd, GLA_HEADS, GLA_K_DIM, GLA_V_DIM), F32)],
        input_output_aliases=aliases,
        scratch_shapes=[pltpu.VMEM((rows, kw), F32), pltpu.VMEM((rows, kw), F32), pltpu.VMEM((rows, vw), BF16),
                        pltpu.VMEM((rows, kw), F32), pltpu.VMEM((rows, vw), F32),
                        pltpu.VMEM((GLA_HEADS, GLA_V_DIM, GLA_K_DIM), F32)],
        compiler_params=_cparams(("arbitrary", "arbitrary")),
        name="odd_layer",
    )(*args)


def _rope_tables(positions):
    half = ROT_DIM // 2
    inv_freq = jnp.power(jnp.float32(ROPE_THETA), -jnp.arange(0, ROT_DIM, 2, dtype=jnp.float32) / ROT_DIM)
    ang = positions.astype(jnp.float32)[:, None] * inv_freq[None, :]
    cos, sin = jnp.cos(ang), jnp.sin(ang)
    n = positions.shape[0]
    ones = jnp.ones((n, DA_QK_DIM - ROT_DIM), F32)
    zeros = jnp.zeros((n, DA_QK_DIM - ROT_DIM), F32)
    zh = jnp.zeros((n, half), F32)
    cos_map = jnp.concatenate([cos, cos, ones], axis=1)
    sina_map = jnp.concatenate([zh, sin, zeros], axis=1)
    sinb_map = jnp.concatenate([-sin, zh, zeros], axis=1)
    two = lambda t: jnp.concatenate([t, t], axis=1)
    return two(cos_map), two(sina_map), two(sinb_map)


def kernel(x_prompt, x_sample, cache_k, cache_v, state_pool, state_gla, page_table, w_in_even, w_pool_lin,
           pool_scale, diff_lambda_params, subln_w, w_out_even, w_in_odd, w_gate_a, w_gate_b, b_gate, gla_norm_w,
           w_out_odd, ln_g, ln_b):
    n_p, t_p, _ = x_prompt.shape
    n_s, t_s, _ = x_sample.shape
    n_pool, n_even, page_size = cache_k.shape[:3]
    n_pages = page_table.shape[1]
    past_len = n_pages * page_size

    tr_p = min(512, t_p)
    tq_p = min(512, t_p)
    tn_even_s = 16
    tn_odd_s = GLA_BLOCK // t_s

    w_in_even_b = w_in_even.astype(BF16)
    w_pool_b = w_pool_lin.astype(BF16)
    w_out_even_b = w_out_even.astype(BF16)
    w_in_odd_b = w_in_odd.astype(BF16)
    w_out_odd_b = w_out_odd.astype(BF16)
    pad_rank = LANES - GLA_GATE_RANK
    w_ga_b = jnp.pad(w_gate_a, ((0, 0), (0, 0), (0, pad_rank))).astype(BF16)
    w_gb_b = jnp.pad(w_gate_b, ((0, 0), (0, pad_rank), (0, 0))).astype(BF16)

    cos_p, sina_p, sinb_p = _rope_tables(jnp.arange(t_p))
    cos_s, sina_s, sinb_s = (jnp.tile(t, (tn_even_s, 1)) for t in _rope_tables(past_len + jnp.arange(t_s)))

    ck = cache_k.reshape(n_pool, n_even, page_size * DA_HEADS, 2 * DA_QK_DIM)
    cv = cache_v.reshape(n_pool, n_even, page_size * DA_HEADS, DA_V_DIM)
    pt_flat = page_table.reshape(-1).astype(jnp.int32)
    pool16 = jnp.pad(state_pool, ((0, 0), (0, 0), (POOL_HIST - state_pool.shape[2], 0), (0, 0)))

    yp = x_prompt.reshape(n_p * t_p, D_MODEL)
    ys = x_sample.reshape(n_s * t_s, D_MODEL)
    n_odd = state_gla.shape[1]
    pp, psm = [], []
    kv_p = kv_s = gla_p = gla_s = None
    for i in range(DEPTH):
        j = i // 2
        lng, lnb = ln_g[i][None, :], ln_b[i][None, :]
        if i % 2 == 0:
            lam_init = 0.8 - 0.6 * math.exp(-0.3 * i)
            pscale = pool_scale[j][None, :]
            sub_w = subln_w[j][None, :]
            lam = diff_lambda_params[j]
            pmix, q, k32, v32, ga, npool, kbf, vt = _even_in(
                yp, n_p, t_p, 1, tr_p, tq_p, 0, w_in_even_b[j], w_pool_b[j], pscale, cos_p, sina_p, sinb_p,
                None, j, n_even, kv_p, BF16)
            kv_p = (k32, v32)
            omix = _attn_flat(q, kbf, vt, ga, lam, sub_w, n_p, t_p, tq_p, lam_init)
            yp = _out_ln(pmix, omix, yp, w_out_even_b[j], lng, lnb, min(2 * tr_p, t_p))
            pp.append(npool[:, 1:])
            pmix, q, k32, v32, ga, npool = _even_in(
                ys, n_s, t_s, tn_even_s, t_s, t_s, past_len, w_in_even_b[j], w_pool_b[j], pscale, cos_s, sina_s,
                sinb_s, pool16, j, n_even, kv_s, F32)
            kv_s = (k32, v32)
            omix = _attn_sample(pt_flat, q, ck, cv, k32, v32, ga, lam, sub_w, j, n_s, t_s, n_pages, lam_init)
            ys = _out_ln(pmix, omix, ys, w_out_even_b[j], lng, lnb, min(512, n_s * t_s))
            psm.append(npool[:, 1:])
        else:
            ow = (w_in_odd_b[j], w_ga_b[j], w_gb_b[j], b_gate[j][None, :], gla_norm_w[j][None, :], w_out_odd_b[j],
                  lng, lnb)
            yp, gla_p = _odd_layer(yp, n_p, t_p, 1, tr_p, *ow, None, j, n_odd, gla_p)
            ys, gla_s = _odd_layer(ys, n_s, t_s, tn_odd_s, t_s, *ow, state_gla, j, n_odd, gla_s)
    kv5 = lambda a, n, t: a.reshape(n, n_even, t, DA_HEADS, LANES)
    return (yp.reshape(n_p, t_p, D_MODEL), ys.reshape(n_s, t_s, D_MODEL),
            kv5(kv_p[0], n_p, t_p), kv5(kv_p[1], n_p, t_p), jnp.stack(pp, axis=1), gla_p,
            kv5(kv_s[0], n_s, t_s), kv5(kv_s[1], n_s, t_s), jnp.stack(psm, axis=1), gla_s)
```

```python
import functools
import math

import numpy as np
import jax
import jax.numpy as jnp
from jax import lax
from jax.experimental import pallas as pl
from jax.experimental.pallas import tpu as pltpu

F32 = jnp.float32
BF16 = jnp.bfloat16

LANES = 128
SUBLANES = 8
VMEM_LIMIT_BYTES = 56 * 1024 * 1024

D_MODEL = 1024
DEPTH = 4
POOL_WIDTH = 512
POOL_WINDOWS = (2, 4, 8, 16)
POOL_GROUP_DIM = 128
POOL_HIST = 16
DA_HEADS = 4
DA_QK_DIM = 64
DA_V_DIM = 128
DA_WIDTH = 512
ROT_DIM = 16
ROPE_THETA = 500000.0
GLA_HEADS = 4
GLA_K_DIM = 128
GLA_V_DIM = 256
GLA_GATE_RANK = 16
GLA_TAU = 16.0
GLA_BLOCK = 64
DN_ALPHA = (2 * DEPTH) ** 0.25
LN_EPS = 1e-5
RMS_EPS = 1e-5

_NT = (((1,), (1,)), ((), ()))
_TN = (((0,), (0,)), ((), ()))


def _dot(a, b):
    return jnp.dot(a, b, preferred_element_type=F32)


def _dot_nt(a, b):
    return lax.dot_general(a, b, _NT, preferred_element_type=F32)


def _dot_tn(a, b):
    return lax.dot_general(a, b, _TN, preferred_element_type=F32)


def _silu(x):
    return x * jax.nn.sigmoid(x)


def _cparams(sem):
    return pltpu.CompilerParams(dimension_semantics=sem, vmem_limit_bytes=VMEM_LIMIT_BYTES)


def _row_groups(rows, group=256):
    return [slice(r0, min(r0 + group, rows)) for r0 in range(0, rows, group)]


def _layer_norm_rows(z, g, b):
    mu = jnp.mean(z, axis=-1, keepdims=True)
    zc = z - mu
    var = jnp.mean(zc * zc, axis=-1, keepdims=True)
    return zc * lax.rsqrt(var + LN_EPS) * g + b


def _own_layer(out_ref, layer_j, n_alias):
    if n_alias:
        return ()
    for jj in range(out_ref.shape[1]):
        if jj != layer_j:
            out_ref[:, jj] = jnp.zeros(out_ref.shape[:1] + out_ref.shape[2:], out_ref.dtype)
    return (layer_j,)


def _store_head_rows(out_ref, layer, val, h, tn, tr):
    for s in range(tn):
        out_ref[(s,) + layer + (pl.ds(h, tr, stride=DA_HEADS), slice(None))] = val[s * tr:(s + 1) * tr, :]


def _even_in_kernel(*refs, tn, tr, tk, pos0, has_past, layer_j, n_alias):
    n_in = 7 + (1 if has_past else 0)
    refs = refs[:n_in] + refs[n_in + n_alias:]
    if has_past:
        (x_ref, w_ref, wlin_ref, pscale_ref, cos_ref, sina_ref, sinb_ref, buf_ref,
         pmix_ref, q_ref, k32_ref, v32_ref, ga_ref, npool_ref, ext_ref) = refs
        kbf_ref = vt_ref = None
    else:
        (x_ref, w_ref, wlin_ref, pscale_ref, cos_ref, sina_ref, sinb_ref,
         pmix_ref, q_ref, k32_ref, v32_ref, ga_ref, npool_ref, kbf_ref, vt_ref, ext_ref) = refs
        buf_ref = None
    k_layer = _own_layer(k32_ref, layer_j, n_alias)
    v_layer = _own_layer(v32_ref, layer_j, n_alias)
    r = pl.program_id(1)
    rows = tn * tr
    xb = x_ref[...].astype(BF16)

    def proj(c0, n):
        return _dot(xb, w_ref[:, c0:c0 + n])

    pu = proj(0, POOL_WIDTH)
    pg = proj(POOL_WIDTH, POOL_WIDTH)
    if has_past:
        ext_ref[:, 0:POOL_HIST, :] = buf_ref[...]
    else:
        @pl.when(r == 0)
        def _():
            ext_ref[:, 0:POOL_HIST, :] = jnp.zeros((tn, POOL_HIST, POOL_WIDTH), F32)

        @pl.when(r > 0)
        def _():
            ext_ref[:, 0:POOL_HIST, :] = ext_ref[:, tr:tr + POOL_HIST, :]
    for s in range(tn):
        ext_ref[s, POOL_HIST:POOL_HIST + tr, :] = pu[s * tr:(s + 1) * tr, :]
    npool_ref[...] = ext_ref[:, tr:tr + POOL_HIST, :]

    pos = pos0 + r * tr + lax.broadcasted_iota(jnp.int32, (tr, POOL_GROUP_DIM), 0)
    for g, w in enumerate(POOL_WINDOWS):
        c0 = g * POOL_GROUP_DIM
        cnt = jnp.minimum(pos + 1, w).astype(F32)
        pieces = []
        for s in range(tn):
            u_g = ext_ref[s, POOL_HIST:POOL_HIST + tr, c0:c0 + POOL_GROUP_DIM]
            acc = u_g
            for i in range(1, w):
                acc = acc + ext_ref[s, POOL_HIST - i:POOL_HIST - i + tr, c0:c0 + POOL_GROUP_DIM]
            pieces.append(acc / cnt - u_g)
        d = pieces[0] if tn == 1 else jnp.concatenate(pieces, axis=0)
        y = _dot(d.astype(BF16), wlin_ref[g]) * pscale_ref[:, c0:c0 + POOL_GROUP_DIM]
        pmix_ref[:, c0:c0 + POOL_GROUP_DIM] = (y * _silu(pg[:, c0:c0 + POOL_GROUP_DIM])).astype(pmix_ref.dtype)

    cos = cos_ref[...]
    sina = sina_ref[...]
    sinb = sinb_ref[...]

    def rope(xh):
        return (xh * cos + pltpu.roll(xh, ROT_DIM // 2, 1) * sina
                + pltpu.roll(xh, LANES - ROT_DIM // 2, 1) * sinb)

    q = proj(2 * POOL_WIDTH, DA_WIDTH)
    k = proj(2 * POOL_WIDTH + DA_WIDTH, DA_WIDTH)
    for h in range(DA_HEADS):
        sl = slice(h * LANES, (h + 1) * LANES)
        q_scale = DA_QK_DIM ** -0.5 * (1.0 if has_past else math.log2(math.e))
        q_ref[:, sl] = (rope(q[:, sl]) * q_scale).astype(q_ref.dtype)
        kr = rope(k[:, sl])
        _store_head_rows(k32_ref, k_layer, kr, h, tn, tr)
        if kbf_ref is not None:
            kbf_ref[:, sl] = kr.astype(kbf_ref.dtype)
    v = proj(2 * POOL_WIDTH + 2 * DA_WIDTH, DA_WIDTH)
    for h in range(DA_HEADS):
        _store_head_rows(v32_ref, v_layer, v[:, h * LANES:(h + 1) * LANES], h, tn, tr)
    if vt_ref is not None:
        vt = v.T.astype(vt_ref.dtype)
        for h in range(DA_HEADS):
            for c in range(tr // tk):
                vt_ref[h, c] = vt[h * DA_V_DIM:(h + 1) * DA_V_DIM, c * tk:(c + 1) * tk]
    ga_ref[...] = _silu(proj(2 * POOL_WIDTH + 3 * DA_WIDTH, DA_WIDTH)).astype(ga_ref.dtype)


def _even_in(x2d, n_seq, t_seq, tn, tr, tk, pos0, w_in, wlin, pscale, cos, sina, sinb, buf16, layer_j,
             n_even, kv_prev, act_dtype):
    n_row_tiles = t_seq // tr
    n_seq_tiles = n_seq // tn
    rows = tn * tr
    total = n_seq * t_seq
    has_past = buf16 is not None
    n_alias = 0 if kv_prev is None else 2
    kern = functools.partial(_even_in_kernel, tn=tn, tr=tr, tk=tk, pos0=pos0, has_past=has_past,
                             layer_j=layer_j, n_alias=n_alias)
    row_map = lambda n, r: (n * n_row_tiles + r, 0)
    const2 = lambda n, r: (0, 0)
    tab_map = (lambda n, r: (r, 0)) if not has_past else const2
    in_specs = [
        pl.BlockSpec((rows, D_MODEL), row_map),
        pl.BlockSpec(w_in.shape, const2),
        pl.BlockSpec(wlin.shape, lambda n, r: (0, 0, 0)),
        pl.BlockSpec(pscale.shape, const2),
        pl.BlockSpec((rows, LANES), tab_map),
        pl.BlockSpec((rows, LANES), tab_map),
        pl.BlockSpec((rows, LANES), tab_map),
    ]
    args = [x2d, w_in, wlin, pscale, cos, sina, sinb]
    blk = lambda: pl.BlockSpec((rows, DA_WIDTH), row_map)
    sds = lambda dt: jax.ShapeDtypeStruct((total, DA_WIDTH), dt)
    if kv_prev is None:
        kv_blk = lambda: pl.BlockSpec((tn, n_even, tr * DA_HEADS, LANES), lambda n, r: (n, 0, r, 0))
    else:
        kv_blk = lambda: pl.BlockSpec((tn, None, tr * DA_HEADS, LANES), lambda n, r: (n, layer_j, r, 0))
    kv_sds = jax.ShapeDtypeStruct((n_seq, n_even, t_seq * DA_HEADS, LANES), F32)
    out_specs = [blk(), blk(), kv_blk(), kv_blk(), blk(),
                 pl.BlockSpec((tn, POOL_HIST, POOL_WIDTH), lambda n, r: (n, 0, 0))]
    out_shape = [sds(act_dtype), sds(act_dtype), kv_sds, kv_sds, sds(act_dtype),
                 jax.ShapeDtypeStruct((n_seq, POOL_HIST, POOL_WIDTH), F32)]
    if has_past:
        in_specs.append(pl.BlockSpec((tn, None, POOL_HIST, POOL_WIDTH), lambda n, r: (n, layer_j, 0, 0)))
        args.append(buf16)
    else:
        out_specs += [blk(), pl.BlockSpec((None, DA_HEADS, tr // tk, DA_V_DIM, tk), lambda n, r: (n, 0, r, 0, 0))]
        out_shape += [sds(BF16), jax.ShapeDtypeStruct((n_seq, DA_HEADS, t_seq // tk, DA_V_DIM, tk), BF16)]
    aliases = {}
    if kv_prev is not None:
        aliases = {len(args): 2, len(args) + 1: 3}
        in_specs += [pl.BlockSpec(memory_space=pl.ANY), pl.BlockSpec(memory_space=pl.ANY)]
        args += list(kv_prev)
    return pl.pallas_call(
        kern,
        grid=(n_seq_tiles, n_row_tiles),
        in_specs=in_specs,
        out_specs=out_specs,
        out_shape=out_shape,
        input_output_aliases=aliases,
        scratch_shapes=[pltpu.VMEM((tn, POOL_HIST + tr, POOL_WIDTH), F32)],
        compiler_params=_cparams(("arbitrary", "arbitrary")),
        name="even_in",
    )(*args)


def _lambda_full(lam_ref, lam_init):
    lam = lam_ref[...]
    a = jnp.sum(lam[0:1, :] * lam[1:2, :], axis=1, keepdims=True)
    b = jnp.sum(lam[2:3, :] * lam[3:4, :], axis=1, keepdims=True)
    return jnp.exp(a) - jnp.exp(b) + lam_init


def _diff_finish(o1, o2, lam_full, lam_init, sub_w, gate):
    o = o1 - lam_full * o2
    ms = jnp.mean(o * o, axis=-1, keepdims=True)
    o = o * lax.rsqrt(ms + RMS_EPS) * sub_w * (1.0 - lam_init)
    return o * gate


def _stack_maps(q):
    lane = lax.broadcasted_iota(jnp.int32, q.shape, 1)
    zero = jnp.zeros_like(q)
    return jnp.concatenate([jnp.where(lane < DA_QK_DIM, q, zero), jnp.where(lane >= DA_QK_DIM, q, zero)], axis=0)


def _attn_flat_kernel(q_ref, k_ref, vt_ref, ga_ref, lam_ref, sub_ref, o_ref, acc_ref, s_ref, qs_ref,
                      *, tq, nq, lam_init):
    steps = [(qi, kb) for qi in range(nq) for kb in [qi] + list(range(qi))]
    lam_full = _lambda_full(lam_ref, lam_init)

    def scores(n):
        qi, kb = steps[n]
        if kb == qi:
            qs_ref[qi % 2] = _stack_maps(q_ref[qi * tq:(qi + 1) * tq, :])
        s = _dot_nt(k_ref[kb * tq:(kb + 1) * tq, :], qs_ref[qi % 2])
        if kb == qi:
            key = lax.broadcasted_iota(jnp.int32, s.shape, 0)
            qry = lax.broadcasted_iota(jnp.int32, s.shape, 1)
            qry = jnp.where(qry >= tq, qry - tq, qry)
            s = jnp.where(key <= qry, s, -jnp.inf)
        s_ref[n % 2] = s
        return jnp.max(s, axis=0, keepdims=True)

    m = l = None
    mx = scores(0)
    for n, (qi, kb) in enumerate(steps):
        mx_next = scores(n + 1) if n + 1 < len(steps) else None
        slot = qi % 2
        if kb == qi:
            m = mx
            p = jnp.exp2(s_ref[n % 2] - m)
            l = jnp.sum(p, axis=0, keepdims=True)
            acc_ref[slot] = _dot(vt_ref[kb], p.astype(BF16))
        else:
            m_new = jnp.maximum(m, mx)
            alpha = jnp.exp2(m - m_new)
            p = jnp.exp2(s_ref[n % 2] - m_new)
            l = alpha * l + jnp.sum(p, axis=0, keepdims=True)
            acc_ref[slot] = alpha * acc_ref[slot] + _dot(vt_ref[kb], p.astype(BF16))
            m = m_new
        if n + 1 == len(steps) or steps[n + 1][0] != qi:
            rows = slice(qi * tq, (qi + 1) * tq)
            o_t = acc_ref[slot] / l
            out = _diff_finish(o_t[:, :tq].T, o_t[:, tq:].T, lam_full, lam_init, sub_ref[...],
                               ga_ref[rows, :].astype(F32))
            o_ref[rows, :] = out.astype(o_ref.dtype)
        mx = mx_next


def _attn_flat(q, kbf, vt, ga, lam, sub_w, n_seq, t_seq, tq, lam_init):
    nq = t_seq // tq
    kern = functools.partial(_attn_flat_kernel, tq=tq, nq=nq, lam_init=lam_init)
    seq_head = lambda b, h: (b, h)
    c2 = lambda b, h: (0, 0)
    return pl.pallas_call(
        kern,
        grid=(n_seq, DA_HEADS),
        in_specs=[
            pl.BlockSpec((t_seq, LANES), seq_head),
            pl.BlockSpec((t_seq, LANES), seq_head),
            pl.BlockSpec((None, None, nq, DA_V_DIM, tq), lambda b, h: (b, h, 0, 0, 0)),
            pl.BlockSpec((t_seq, LANES), seq_head),
            pl.BlockSpec(lam.shape, c2),
            pl.BlockSpec(sub_w.shape, c2),
        ],
        out_specs=pl.BlockSpec((t_seq, LANES), seq_head),
        out_shape=jax.ShapeDtypeStruct((n_seq * t_seq, DA_WIDTH), BF16),
        scratch_shapes=[pltpu.VMEM((2, DA_V_DIM, 2 * tq), F32), pltpu.VMEM((2, tq, 2 * tq), F32),
                        pltpu.VMEM((2, 2 * tq, LANES), BF16)],
        compiler_params=_cparams(("arbitrary", "arbitrary")),
        name="attn_prompt",
    )(q, kbf, vt, ga, lam, sub_w)


def _attn_sample_kernel(*refs, t_new, n_pages, lam_init):
    pt_ref, q_ref = refs[0], refs[1]
    k_refs = refs[2:2 + n_pages]
    v_refs = refs[2 + n_pages:2 + 2 * n_pages]
    (kn_ref, vn_ref, ga_ref, lam_ref, sub_ref, biasn_ref, o_ref, s_ref) = refs[2 + 2 * n_pages:]
    del pt_ref
    page = k_refs[0].shape[0] // DA_HEADS
    lam_full = _lambda_full(lam_ref, lam_init)
    kn = kn_ref[...].astype(BF16)
    vn = vn_ref[...].astype(BF16)

    for h in range(DA_HEADS):
        hs = slice(h * LANES, (h + 1) * LANES)
        head_rows = pl.ds(h, page, stride=DA_HEADS)
        wq = _stack_maps(q_ref[:, hs]).astype(BF16)
        mx = None
        for p in range(n_pages):
            s = _dot_nt(wq, k_refs[p][head_rows, :].astype(BF16))
            s_ref[h, :, p * page:(p + 1) * page] = s
            mx = s if mx is None else jnp.maximum(mx, s)
        sn = _dot_nt(wq, kn) + biasn_ref[h]
        m = jnp.maximum(jnp.max(mx, axis=1, keepdims=True), jnp.max(sn, axis=1, keepdims=True))

        pn = jnp.exp(sn - m)
        acc = _dot(pn.astype(BF16), vn)
        lsum = None
        for p in range(n_pages):
            pe = jnp.exp(s_ref[h, :, p * page:(p + 1) * page] - m)
            lsum = pe if lsum is None else lsum + pe
            acc = acc + _dot(pe.astype(BF16), v_refs[p][head_rows, :].astype(BF16))
        l = jnp.sum(lsum, axis=1, keepdims=True) + jnp.sum(pn, axis=1, keepdims=True)

        o1 = acc[:t_new] / l[:t_new]
        o2 = acc[t_new:] / l[t_new:]
        out = _diff_finish(o1, o2, lam_full, lam_init, sub_ref[...], ga_ref[:, hs])
        o_ref[:, hs] = out.astype(o_ref.dtype)


def _sample_new_bias(t_new):
    row_q = (np.arange(2 * t_new) % t_new)[None, :, None]
    col = np.arange(DA_HEADS * t_new)[None, None, :]
    head = np.arange(DA_HEADS)[:, None, None]
    ok = (col % DA_HEADS == head) & (col // DA_HEADS <= row_q)
    return jnp.asarray(np.where(ok, 0.0, -np.inf).astype(np.float32))


def _attn_sample(page_table_flat, q, ck, cv, k_new, v_new, ga, lam, sub_w, layer_j, n_seq, t_new, n_pages,
                 lam_init):
    page_rows = ck.shape[2]
    kern = functools.partial(_attn_sample_kernel, t_new=t_new, n_pages=n_pages, lam_init=lam_init)
    biasn = _sample_new_bias(t_new)
    seq_map = lambda n, pt: (n, 0)
    const2 = lambda n, pt: (0, 0)

    def page_spec(p):
        return pl.BlockSpec((None, None, page_rows, LANES), lambda n, pt: (pt[n * n_pages + p], layer_j, 0, 0))

    seq_spec = lambda: pl.BlockSpec((t_new, DA_WIDTH), seq_map)
    new_spec = lambda: pl.BlockSpec((None, None, t_new * DA_HEADS, LANES), lambda n, pt: (n, layer_j, 0, 0))
    grid_spec = pltpu.PrefetchScalarGridSpec(
        num_scalar_prefetch=1,
        grid=(n_seq,),
        in_specs=([seq_spec()] + [page_spec(p) for p in range(n_pages)] + [page_spec(p) for p in range(n_pages)]
                  + [new_spec(), new_spec(), seq_spec(), pl.BlockSpec(lam.shape, const2),
                     pl.BlockSpec(sub_w.shape, const2), pl.BlockSpec(biasn.shape, lambda n, pt: (0, 0, 0))]),
        out_specs=seq_spec(),
        scratch_shapes=[pltpu.VMEM((DA_HEADS, 2 * t_new, n_pages * page_rows // DA_HEADS), F32)],
    )
    return pl.pallas_call(
        kern,
        grid_spec=grid_spec,
        out_shape=jax.ShapeDtypeStruct((n_seq * t_new, DA_WIDTH), F32),
        compiler_params=_cparams(("arbitrary",)),
        name="attn_sample",
    )(page_table_flat, q, *([ck] * n_pages), *([cv] * n_pages), k_new, v_new, ga, lam, sub_w, biasn)


def _out_ln_kernel(a_ref, b_ref, x_ref, w_ref, g_ref, bb_ref, y_ref):
    half = a_ref.shape[1]
    for rs in _row_groups(a_ref.shape[0]):
        acc = (_dot(a_ref[rs, :].astype(BF16), w_ref[0:half, :])
               + _dot(b_ref[rs, :].astype(BF16), w_ref[half:2 * half, :]))
        y_ref[rs, :] = _layer_norm_rows(DN_ALPHA * x_ref[rs, :] + acc, g_ref[...], bb_ref[...])


def _out_ln(a, b, x2d, w_out, ln_g, ln_b, tm):
    total = x2d.shape[0]
    row = lambda i: (i, 0)
    const = lambda i: (0, 0)
    return pl.pallas_call(
        _out_ln_kernel,
        grid=(total // tm,),
        in_specs=[
            pl.BlockSpec((tm, a.shape[1]), row),
            pl.BlockSpec((tm, b.shape[1]), row),
            pl.BlockSpec((tm, D_MODEL), row),
            pl.BlockSpec(w_out.shape, const),
            pl.BlockSpec(ln_g.shape, const),
            pl.BlockSpec(ln_b.shape, const),
        ],
        out_specs=pl.BlockSpec((tm, D_MODEL), row),
        out_shape=jax.ShapeDtypeStruct((total, D_MODEL), F32),
        compiler_params=_cparams(("arbitrary",)),
        name="out_ln",
    )(a, b, x2d, w_out, ln_g, ln_b)


def _gla_constants(top):
    big = GLA_BLOCK
    idx = np.arange(big)
    levels = int(round(math.log2(top)))
    seg = [((idx[:, None] // top == idx[None, :] // top) & (idx[None, :] <= idx[:, None])).astype(np.float32)]
    pairs = []
    n = top
    for _ in range(levels):
        start = (idx // n) * n
        rho = start + n // 2 - 1
        upper = idx > rho
        t = idx[None, :]
        c = np.where(upper[:, None], (t > rho[:, None]) & (t <= idx[:, None]), (t > idx[:, None]) & (t <= rho[:, None]))
        seg.append(c.astype(np.float32))
        pairs.append(((start[:, None] == start[None, :]) & upper[:, None] & ~upper[None, :]).astype(np.float32))
        n //= 2
    seg = np.concatenate(seg, axis=0)
    return jnp.asarray(np.concatenate([seg, seg], axis=1), BF16), jnp.asarray(np.stack(pairs)), levels


def _gla_intra(q_h, k_h, seg_h, pair_ref, levels):
    big = GLA_BLOCK
    att = jnp.zeros((big, big), F32)
    for lv in range(levels):
        e = jnp.exp2(seg_h[(1 + lv) * big:(2 + lv) * big, :])
        att = att + _dot_nt((q_h * e).astype(BF16), (k_h * e).astype(BF16)) * pair_ref[lv]
    diag = jnp.sum(q_h * k_h, axis=1, keepdims=True)
    row = lax.broadcasted_iota(jnp.int32, (big, big), 0)
    col = lax.broadcasted_iota(jnp.int32, (big, big), 1)
    return att + jnp.where(row == col, diag, 0.0)


def _odd_kernel(*refs, tn, tr, levels, has_state, layer_j, n_alias):
    n_in = 11 + (1 if has_state else 0)
    refs = refs[:n_in] + refs[n_in + n_alias:]
    if has_state:
        (x_ref, w_ref, wga_ref, wgb_ref, bg_ref, nw_ref, wo_ref, lng_ref, lnb_ref, seg_ref, pair_ref,
         s0_ref, y_ref, sout_ref, q_s, k_s, v_s, g_s, o_s, st_s) = refs
    else:
        (x_ref, w_ref, wga_ref, wgb_ref, bg_ref, nw_ref, wo_ref, lng_ref, lnb_ref, seg_ref, pair_ref,
         y_ref, sout_ref, q_s, k_s, v_s, g_s, o_s, st_s) = refs
        s0_ref = None
    s_layer = _own_layer(sout_ref, layer_j, n_alias)
    r = pl.program_id(1)
    rows = tn * tr
    big = GLA_BLOCK
    kw = GLA_HEADS * GLA_K_DIM
    vw = GLA_HEADS * GLA_V_DIM
    x = x_ref[...]
    xb = x.astype(BF16)
    q_s[...] = _dot(xb, w_ref[:, 0:kw]) * (GLA_K_DIM ** -0.5)
    k_s[...] = _dot(xb, w_ref[:, kw:2 * kw])
    v_s[...] = _dot(xb, w_ref[:, 2 * kw:2 * kw + vw]).astype(BF16)
    low = _dot(xb, wga_ref[...])
    z = _dot(low.astype(BF16), wgb_ref[...]) + bg_ref[...]
    g_s[...] = (jnp.minimum(z, 0.0) - jnp.log1p(jnp.exp(-jnp.abs(z)))) * (math.log2(math.e) / GLA_TAU)

    if not has_state:
        @pl.when(r == 0)
        def _():
            st_s[...] = jnp.zeros(st_s.shape, F32)

    def block(c, carry):
        r0 = pl.multiple_of(c * big, big)
        qb = q_s[pl.ds(r0, big), :]
        kb = k_s[pl.ds(r0, big), :]
        gb = g_s[pl.ds(r0, big), :]
        hi = gb.astype(BF16)
        lo = (gb - hi.astype(F32)).astype(BF16)
        seg = _dot(seg_ref[...], jnp.concatenate([hi, lo], axis=0))
        for h in range(GLA_HEADS):
            ks = slice(h * GLA_K_DIM, (h + 1) * GLA_K_DIM)
            q_h, k_h, seg_h = qb[:, ks], kb[:, ks], seg[:, ks]
            v_h = v_s[pl.ds(r0, big), h * GLA_V_DIM:(h + 1) * GLA_V_DIM]
            att = _gla_intra(q_h, k_h, seg_h, pair_ref, levels)
            o_h = _dot(att.astype(BF16), v_h)
            cb = seg_h[0:big, :]
            qe = q_h * jnp.exp2(cb)
            if has_state:
                n_runs = big // tr
                v_hf = v_h.astype(F32)
                pad_k = jnp.zeros((2 * SUBLANES - tr, GLA_K_DIM), F32)
                pad_v = jnp.zeros((2 * SUBLANES - tr, GLA_V_DIM), F32)
                inter = []
                for s in range(n_runs):
                    rs = slice(s * tr, (s + 1) * tr)
                    seq = c * n_runs + s
                    st = s0_ref[seq, h]
                    qep = jnp.concatenate([qe[rs], pad_k], axis=0).astype(BF16)
                    inter.append(_dot(qep, st.astype(BF16))[0:tr])
                    cb_last = cb[(s + 1) * tr - 1:(s + 1) * tr, :]
                    kt = k_h[rs] * jnp.exp2(cb_last - cb[rs])
                    ktp = jnp.concatenate([kt, pad_k], axis=0).astype(BF16)
                    vtp = jnp.concatenate([v_hf[rs], pad_v], axis=0).astype(BF16)
                    dcol = jnp.broadcast_to(jnp.exp2(cb_last), (GLA_K_DIM, GLA_K_DIM)).T
                    decay = jnp.concatenate([dcol] * (GLA_V_DIM // GLA_K_DIM), axis=1)
                    sout_ref[(seq,) + s_layer + (h,)] = st * decay + _dot_tn(ktp, vtp)
                o_h = o_h + jnp.concatenate(inter, axis=0)
            else:
                st = st_s[h]
                o_h = o_h + _dot_nt(qe.astype(BF16), st.astype(BF16))
                cb_last = cb[big - 1:big, :]
                kt = (k_h * jnp.exp2(cb_last - cb)).astype(BF16)
                st_s[h] = st * jnp.exp2(cb_last) + _dot_tn(v_h, kt)
            o_s[pl.ds(r0, big), h * GLA_V_DIM:(h + 1) * GLA_V_DIM] = o_h
        return carry

    lax.fori_loop(0, rows // big, block, 0, unroll=4 if rows // big >= 4 else 1)

    if not has_state:
        @pl.when(r == pl.num_programs(1) - 1)
        def _():
            for h in range(GLA_HEADS):
                sout_ref[(0,) + s_layer + (h,)] = st_s[h].T

    nw = nw_ref[...]
    for h in range(GLA_HEADS):
        vs = slice(h * GLA_V_DIM, (h + 1) * GLA_V_DIM)
        o_h = o_s[:, vs]
        ms = jnp.mean(o_h * o_h, axis=-1, keepdims=True)
        gate = _dot(xb, w_ref[:, 2 * kw + vw + h * GLA_V_DIM:2 * kw + vw + (h + 1) * GLA_V_DIM])
        o_s[:, vs] = o_h * lax.rsqrt(ms + RMS_EPS) * nw * _silu(gate)
    for rs in _row_groups(rows):
        acc = _dot(o_s[rs, :].astype(BF16), wo_ref[...])
        y_ref[rs, :] = _layer_norm_rows(DN_ALPHA * x_ref[rs, :] + acc, lng_ref[...], lnb_ref[...])


def _odd_layer(x2d, n_seq, t_seq, tn, tr, w_in, wga, wgb, bg, nw, wo, lng, lnb, state0, layer_j, n_odd,
               state_prev):
    has_state = state0 is not None
    seg, pair, levels = _gla_constants(tr if has_state else GLA_BLOCK)
    n_row_tiles = t_seq // tr
    rows = tn * tr
    total = n_seq * t_seq
    n_alias = 0 if state_prev is None else 1
    kern = functools.partial(_odd_kernel, tn=tn, tr=tr, levels=levels, has_state=has_state, layer_j=layer_j,
                             n_alias=n_alias)
    row_map = lambda n, r: (n * n_row_tiles + r, 0)
    c2 = lambda n, r: (0, 0)
    c3 = lambda n, r: (0, 0, 0)
    st_map = lambda n, r: (n, layer_j, 0, 0, 0)
    in_specs = [
        pl.BlockSpec((rows, D_MODEL), row_map),
        pl.BlockSpec(w_in.shape, c2), pl.BlockSpec(wga.shape, c2), pl.BlockSpec(wgb.shape, c2),
        pl.BlockSpec(bg.shape, c2), pl.BlockSpec(nw.shape, c2), pl.BlockSpec(wo.shape, c2),
        pl.BlockSpec(lng.shape, c2), pl.BlockSpec(lnb.shape, c2),
        pl.BlockSpec(seg.shape, c2), pl.BlockSpec(pair.shape, c3),
    ]
    args = [x2d, w_in, wga, wgb, bg, nw, wo, lng, lnb, seg, pair]
    st_block = (tn, None, GLA_HEADS, GLA_K_DIM, GLA_V_DIM)
    if has_state:
        in_specs.append(pl.BlockSpec(st_block, st_map))
        args.append(state0)
    aliases = {}
    if state_prev is None:
        out_state = pl.BlockSpec((tn, n_odd, GLA_HEADS, GLA_K_DIM, GLA_V_DIM), lambda n, r: (n, 0, 0, 0, 0))
    else:
        out_state = pl.BlockSpec(st_block, st_map)
        aliases = {len(args): 1}
        in_specs.append(pl.BlockSpec(memory_space=pl.ANY))
        args.append(state_prev)
    kw = GLA_HEADS * GLA_K_DIM
    vw = GLA_HEADS * GLA_V_DIM
    return pl.pallas_call(
        kern,
        grid=(n_seq // tn, n_row_tiles),
        in_specs=in_specs,
        out_specs=[pl.BlockSpec((rows, D_MODEL), row_map), out_state],
        out_shape=[jax.ShapeDtypeStruct((total, D_MODEL), F32),
                   jax.ShapeDtypeStruct((n_seq, n_odd, GLA_HEADS, GLA_K_DIM, GLA_V_DIM), F32)],
        input_output_aliases=aliases,
        scratch_shapes=[pltpu.VMEM((rows, kw), F32), pltpu.VMEM((rows, kw), F32), pltpu.VMEM((rows, vw), BF16),
                        pltpu.VMEM((rows, kw), F32), pltpu.VMEM((rows, vw), F32),
                        pltpu.VMEM((GLA_HEADS, GLA_V_DIM, GLA_K_DIM), F32)],
        compiler_params=_cparams(("arbitrary", "arbitrary")),
        name="odd_layer",
    )(*args)


def _rope_tables(positions):
    half = ROT_DIM // 2
    inv_freq = jnp.power(jnp.float32(ROPE_THETA), -jnp.arange(0, ROT_DIM, 2, dtype=jnp.float32) / ROT_DIM)
    ang = positions.astype(jnp.float32)[:, None] * inv_freq[None, :]
    cos, sin = jnp.cos(ang), jnp.sin(ang)
    n = positions.shape[0]
    ones = jnp.ones((n, DA_QK_DIM - ROT_DIM), F32)
    zeros = jnp.zeros((n, DA_QK_DIM - ROT_DIM), F32)
    zh = jnp.zeros((n, half), F32)
    cos_map = jnp.concatenate([cos, cos, ones], axis=1)
    sina_map = jnp.concatenate([zh, sin, zeros], axis=1)
    sinb_map = jnp.concatenate([-sin, zh, zeros], axis=1)
    two = lambda t: jnp.concatenate([t, t], axis=1)
    return two(cos_map), two(sina_map), two(sinb_map)


def kernel(x_prompt, x_sample, cache_k, cache_v, state_pool, state_gla, page_table, w_in_even, w_pool_lin,
           pool_scale, diff_lambda_params, subln_w, w_out_even, w_in_odd, w_gate_a, w_gate_b, b_gate, gla_norm_w,
           w_out_odd, ln_g, ln_b):
    n_p, t_p, _ = x_prompt.shape
    n_s, t_s, _ = x_sample.shape
    n_pool, n_even, page_size = cache_k.shape[:3]
    n_pages = page_table.shape[1]
    past_len = n_pages * page_size

    tr_p = min(512, t_p)
    tq_p = min(512, t_p)
    tn_even_s = 16
    tn_odd_s = GLA_BLOCK // t_s

    w_in_even_b = w_in_even.astype(BF16)
    w_pool_b = w_pool_lin.astype(BF16)
    w_out_even_b = w_out_even.astype(BF16)
    w_in_odd_b = w_in_odd.astype(BF16)
    w_out_odd_b = w_out_odd.astype(BF16)
    pad_rank = LANES - GLA_GATE_RANK
    w_ga_b = jnp.pad(w_gate_a, ((0, 0), (0, 0), (0, pad_rank))).astype(BF16)
    w_gb_b = jnp.pad(w_gate_b, ((0, 0), (0, pad_rank), (0, 0))).astype(BF16)

    cos_p, sina_p, sinb_p = _rope_tables(jnp.arange(t_p))
    cos_s, sina_s, sinb_s = (jnp.tile(t, (tn_even_s, 1)) for t in _rope_tables(past_len + jnp.arange(t_s)))

    ck = cache_k.reshape(n_pool, n_even, page_size * DA_HEADS, 2 * DA_QK_DIM)
    cv = cache_v.reshape(n_pool, n_even, page_size * DA_HEADS, DA_V_DIM)
    pt_flat = page_table.reshape(-1).astype(jnp.int32)
    pool16 = jnp.pad(state_pool, ((0, 0), (0, 0), (POOL_HIST - state_pool.shape[2], 0), (0, 0)))

    yp = x_prompt.reshape(n_p * t_p, D_MODEL)
    ys = x_sample.reshape(n_s * t_s, D_MODEL)
    n_odd = state_gla.shape[1]
    pp, psm = [], []
    kv_p = kv_s = gla_p = gla_s = None
    for i in range(DEPTH):
        j = i // 2
        lng, lnb = ln_g[i][None, :], ln_b[i][None, :]
        if i % 2 == 0:
            lam_init = 0.8 - 0.6 * math.exp(-0.3 * i)
            pscale = pool_scale[j][None, :]
            sub_w = subln_w[j][None, :]
            lam = diff_lambda_params[j]
            pmix, q, k32, v32, ga, npool, kbf, vt = _even_in(
                yp, n_p, t_p, 1, tr_p, tq_p, 0, w_in_even_b[j], w_pool_b[j], pscale, cos_p, sina_p, sinb_p,
                None, j, n_even, kv_p, BF16)
            kv_p = (k32, v32)
            omix = _attn_flat(q, kbf, vt, ga, lam, sub_w, n_p, t_p, tq_p, lam_init)
            yp = _out_ln(pmix, omix, yp, w_out_even_b[j], lng, lnb, min(2 * tr_p, t_p))
            pp.append(npool[:, 1:])
            pmix, q, k32, v32, ga, npool = _even_in(
                ys, n_s, t_s, tn_even_s, t_s, t_s, past_len, w_in_even_b[j], w_pool_b[j], pscale, cos_s, sina_s,
                sinb_s, pool16, j, n_even, kv_s, F32)
            kv_s = (k32, v32)
            omix = _attn_sample(pt_flat, q, ck, cv, k32, v32, ga, lam, sub_w, j, n_s, t_s, n_pages, lam_init)
            ys = _out_ln(pmix, omix, ys, w_out_even_b[j], lng, lnb, min(512, n_s * t_s))
            psm.append(npool[:, 1:])
        else:
            ow = (w_in_odd_b[j], w_ga_b[j], w_gb_b[j], b_gate[j][None, :], gla_norm_w[j][None, :], w_out_odd_b[j],
                  lng, lnb)
            yp, gla_p = _odd_layer(yp, n_p, t_p, 1, tr_p, *ow, None, j, n_odd, gla_p)
            ys, gla_s = _odd_layer(ys, n_s, t_s, tn_odd_s, t_s, *ow, state_gla, j, n_odd, gla_s)
    kv5 = lambda a, n, t: a.reshape(n, n_even, t, DA_HEADS, LANES)
    return (yp.reshape(n_p, t_p, D_MODEL), ys.reshape(n_s, t_s, D_MODEL),
            kv5(kv_p[0], n_p, t_p), kv5(kv_p[1], n_p, t_p), jnp.stack(pp, axis=1), gla_p,
            kv5(kv_s[0], n_s, t_s), kv5(kv_s[1], n_s, t_s), jnp.stack(psm, axis=1), gla_s)
```

```python
import functools
import math

import numpy as np
import jax
import jax.numpy as jnp
from jax import lax
from jax.experimental import pallas as pl
from jax.experimental.pallas import tpu as pltpu

F32 = jnp.float32
BF16 = jnp.bfloat16

LANES = 128
SUBLANES = 8
VMEM_LIMIT_BYTES = 56 * 1024 * 1024

D_MODEL = 1024
DEPTH = 4
POOL_WIDTH = 512
POOL_WINDOWS = (2, 4, 8, 16)
POOL_GROUP_DIM = 128
POOL_HIST = 16
DA_HEADS = 4
DA_QK_DIM = 64
DA_V_DIM = 128
DA_WIDTH = 512
ROT_DIM = 16
ROPE_THETA = 500000.0
GLA_HEADS = 4
GLA_K_DIM = 128
GLA_V_DIM = 256
GLA_GATE_RANK = 16
GLA_TAU = 16.0
GLA_BLOCK = 64
DN_ALPHA = (2 * DEPTH) ** 0.25
LN_EPS = 1e-5
RMS_EPS = 1e-5

_NT = (((1,), (1,)), ((), ()))
_TN = (((0,), (0,)), ((), ()))


def _dot(a, b):
    return jnp.dot(a, b, preferred_element_type=F32)


def _dot_nt(a, b):
    return lax.dot_general(a, b, _NT, preferred_element_type=F32)


def _dot_tn(a, b):
    return lax.dot_general(a, b, _TN, preferred_element_type=F32)


def _silu(x):
    return x * jax.nn.sigmoid(x)


def _cparams(sem):
    return pltpu.CompilerParams(dimension_semantics=sem, vmem_limit_bytes=VMEM_LIMIT_BYTES)


def _row_groups(rows, group=256):
    return [slice(r0, min(r0 + group, rows)) for r0 in range(0, rows, group)]


def _layer_norm_rows(z, g, b):
    mu = jnp.mean(z, axis=-1, keepdims=True)
    zc = z - mu
    var = jnp.mean(zc * zc, axis=-1, keepdims=True)
    return zc * lax.rsqrt(var + LN_EPS) * g + b


def _own_layer(out_ref, layer_j, n_alias):
    if n_alias:
        return ()
    for jj in range(out_ref.shape[1]):
        if jj != layer_j:
            out_ref[:, jj] = jnp.zeros(out_ref.shape[:1] + out_ref.shape[2:], out_ref.dtype)
    return (layer_j,)


def _store_head_rows(out_ref, layer, val, h, tn, tr):
    for s in range(tn):
        out_ref[(s,) + layer + (pl.ds(h, tr, stride=DA_HEADS), slice(None))] = val[s * tr:(s + 1) * tr, :]


def _even_in_kernel(*refs, tn, tr, tk, pos0, has_past, layer_j, n_alias):
    n_in = 7 + (1 if has_past else 0)
    refs = refs[:n_in] + refs[n_in + n_alias:]
    if has_past:
        (x_ref, w_ref, wlin_ref, pscale_ref, cos_ref, sina_ref, sinb_ref, buf_ref,
         pmix_ref, q_ref, k32_ref, v32_ref, ga_ref, npool_ref, ext_ref) = refs
        kbf_ref = vt_ref = None
    else:
        (x_ref, w_ref, wlin_ref, pscale_ref, cos_ref, sina_ref, sinb_ref,
         pmix_ref, q_ref, k32_ref, v32_ref, ga_ref, npool_ref, kbf_ref, vt_ref, ext_ref) = refs
        buf_ref = None
    k_layer = _own_layer(k32_ref, layer_j, n_alias)
    v_layer = _own_layer(v32_ref, layer_j, n_alias)
    r = pl.program_id(1)
    rows = tn * tr
    xb = x_ref[...].astype(BF16)

    def proj(c0, n):
        return _dot(xb, w_ref[:, c0:c0 + n])

    pu = proj(0, POOL_WIDTH)
    pg = proj(POOL_WIDTH, POOL_WIDTH)
    if has_past:
        ext_ref[:, 0:POOL_HIST, :] = buf_ref[...]
    else:
        @pl.when(r == 0)
        def _():
            ext_ref[:, 0:POOL_HIST, :] = jnp.zeros((tn, POOL_HIST, POOL_WIDTH), F32)

        @pl.when(r > 0)
        def _():
            ext_ref[:, 0:POOL_HIST, :] = ext_ref[:, tr:tr + POOL_HIST, :]
    for s in range(tn):
        ext_ref[s, POOL_HIST:POOL_HIST + tr, :] = pu[s * tr:(s + 1) * tr, :]
    npool_ref[...] = ext_ref[:, tr:tr + POOL_HIST, :]

    pos = pos0 + r * tr + lax.broadcasted_iota(jnp.int32, (tr, POOL_GROUP_DIM), 0)
    for g, w in enumerate(POOL_WINDOWS):
        c0 = g * POOL_GROUP_DIM
        cnt = jnp.minimum(pos + 1, w).astype(F32)
        pieces = []
        for s in range(tn):
            u_g = ext_ref[s, POOL_HIST:POOL_HIST + tr, c0:c0 + POOL_GROUP_DIM]
            acc = u_g
            for i in range(1, w):
                acc = acc + ext_ref[s, POOL_HIST - i:POOL_HIST - i + tr, c0:c0 + POOL_GROUP_DIM]
            pieces.append(acc / cnt - u_g)
        d = pieces[0] if tn == 1 else jnp.concatenate(pieces, axis=0)
        y = _dot(d.astype(BF16), wlin_ref[g]) * pscale_ref[:, c0:c0 + POOL_GROUP_DIM]
        pmix_ref[:, c0:c0 + POOL_GROUP_DIM] = (y * _silu(pg[:, c0:c0 + POOL_GROUP_DIM])).astype(pmix_ref.dtype)

    cos = cos_ref[...]
    sina = sina_ref[...]
    sinb = sinb_ref[...]

    def rope(xh):
        return (xh * cos + pltpu.roll(xh, ROT_DIM // 2, 1) * sina
                + pltpu.roll(xh, LANES - ROT_DIM // 2, 1) * sinb)

    q = proj(2 * POOL_WIDTH, DA_WIDTH)
    k = proj(2 * POOL_WIDTH + DA_WIDTH, DA_WIDTH)
    for h in range(DA_HEADS):
        sl = slice(h * LANES, (h + 1) * LANES)
        q_scale = DA_QK_DIM ** -0.5 * (1.0 if has_past else math.log2(math.e))
        q_ref[:, sl] = (rope(q[:, sl]) * q_scale).astype(q_ref.dtype)
        kr = rope(k[:, sl])
        _store_head_rows(k32_ref, k_layer, kr, h, tn, tr)
        if kbf_ref is not None:
            kbf_ref[:, sl] = kr.astype(kbf_ref.dtype)
    v = proj(2 * POOL_WIDTH + 2 * DA_WIDTH, DA_WIDTH)
    for h in range(DA_HEADS):
        _store_head_rows(v32_ref, v_layer, v[:, h * LANES:(h + 1) * LANES], h, tn, tr)
    if vt_ref is not None:
        vt = v.T.astype(vt_ref.dtype)
        for h in range(DA_HEADS):
            for c in range(tr // tk):
                vt_ref[h, c] = vt[h * DA_V_DIM:(h + 1) * DA_V_DIM, c * tk:(c + 1) * tk]
    ga_ref[...] = _silu(proj(2 * POOL_WIDTH + 3 * DA_WIDTH, DA_WIDTH)).astype(ga_ref.dtype)


def _even_in(x2d, n_seq, t_seq, tn, tr, tk, pos0, w_in, wlin, pscale, cos, sina, sinb, buf16, layer_j,
             n_even, kv_prev, act_dtype):
    n_row_tiles = t_seq // tr
    n_seq_tiles = n_seq // tn
    rows = tn * tr
    total = n_seq * t_seq
    has_past = buf16 is not None
    n_alias = 0 if kv_prev is None else 2
    kern = functools.partial(_even_in_kernel, tn=tn, tr=tr, tk=tk, pos0=pos0, has_past=has_past,
                             layer_j=layer_j, n_alias=n_alias)
    row_map = lambda n, r: (n * n_row_tiles + r, 0)
    const2 = lambda n, r: (0, 0)
    tab_map = (lambda n, r: (r, 0)) if not has_past else const2
    in_specs = [
        pl.BlockSpec((rows, D_MODEL), row_map),
        pl.BlockSpec(w_in.shape, const2),
        pl.BlockSpec(wlin.shape, lambda n, r: (0, 0, 0)),
        pl.BlockSpec(pscale.shape, const2),
        pl.BlockSpec((rows, LANES), tab_map),
        pl.BlockSpec((rows, LANES), tab_map),
        pl.BlockSpec((rows, LANES), tab_map),
    ]
    args = [x2d, w_in, wlin, pscale, cos, sina, sinb]
    blk = lambda: pl.BlockSpec((rows, DA_WIDTH), row_map)
    sds = lambda dt: jax.ShapeDtypeStruct((total, DA_WIDTH), dt)
    if kv_prev is None:
        kv_blk = lambda: pl.BlockSpec((tn, n_even, tr * DA_HEADS, LANES), lambda n, r: (n, 0, r, 0))
    else:
        kv_blk = lambda: pl.BlockSpec((tn, None, tr * DA_HEADS, LANES), lambda n, r: (n, layer_j, r, 0))
    kv_sds = jax.ShapeDtypeStruct((n_seq, n_even, t_seq * DA_HEADS, LANES), F32)
    out_specs = [blk(), blk(), kv_blk(), kv_blk(), blk(),
                 pl.BlockSpec((tn, POOL_HIST, POOL_WIDTH), lambda n, r: (n, 0, 0))]
    out_shape = [sds(act_dtype), sds(act_dtype), kv_sds, kv_sds, sds(act_dtype),
                 jax.ShapeDtypeStruct((n_seq, POOL_HIST, POOL_WIDTH), F32)]
    if has_past:
        in_specs.append(pl.BlockSpec((tn, None, POOL_HIST, POOL_WIDTH), lambda n, r: (n, layer_j, 0, 0)))
        args.append(buf16)
    else:
        out_specs += [blk(), pl.BlockSpec((None, DA_HEADS, tr // tk, DA_V_DIM, tk), lambda n, r: (n, 0, r, 0, 0))]
        out_shape += [sds(BF16), jax.ShapeDtypeStruct((n_seq, DA_HEADS, t_seq // tk, DA_V_DIM, tk), BF16)]
    aliases = {}
    if kv_prev is not None:
        aliases = {len(args): 2, len(args) + 1: 3}
        in_specs += [pl.BlockSpec(memory_space=pl.ANY), pl.BlockSpec(memory_space=pl.ANY)]
        args += list(kv_prev)
    return pl.pallas_call(
        kern,
        grid=(n_seq_tiles, n_row_tiles),
        in_specs=in_specs,
        out_specs=out_specs,
        out_shape=out_shape,
        input_output_aliases=aliases,
        scratch_shapes=[pltpu.VMEM((tn, POOL_HIST + tr, POOL_WIDTH), F32)],
        compiler_params=_cparams(("arbitrary", "arbitrary")),
        name="even_in",
    )(*args)


def _lambda_full(lam_ref, lam_init):
    lam = lam_ref[...]
    a = jnp.sum(lam[0:1, :] * lam[1:2, :], axis=1, keepdims=True)
    b = jnp.sum(lam[2:3, :] * lam[3:4, :], axis=1, keepdims=True)
    return jnp.exp(a) - jnp.exp(b) + lam_init


def _diff_finish(o1, o2, lam_full, lam_init, sub_w, gate):
    o = o1 - lam_full * o2
    ms = jnp.mean(o * o, axis=-1, keepdims=True)
    o = o * lax.rsqrt(ms + RMS_EPS) * sub_w * (1.0 - lam_init)
    return o * gate


def _stack_maps(q):
    lane = lax.broadcasted_iota(jnp.int32, q.shape, 1)
    zero = jnp.zeros_like(q)
    return jnp.concatenate([jnp.where(lane < DA_QK_DIM, q, zero), jnp.where(lane >= DA_QK_DIM, q, zero)], axis=0)


def _attn_flat_kernel(q_ref, k_ref, vt_ref, ga_ref, lam_ref, sub_ref, o_ref, acc_ref, s_ref, qs_ref,
                      *, tq, nq, lam_init):
    steps = [(qi, kb) for qi in range(nq) for kb in [qi] + list(range(qi))]
    lam_full = _lambda_full(lam_ref, lam_init)

    def scores(n):
        qi, kb = steps[n]
        if kb == qi:
            qs_ref[qi % 2] = _stack_maps(q_ref[qi * tq:(qi + 1) * tq, :])
        s = _dot_nt(k_ref[kb * tq:(kb + 1) * tq, :], qs_ref[qi % 2])
        if kb == qi:
            key = lax.broadcasted_iota(jnp.int32, s.shape, 0)
            qry = lax.broadcasted_iota(jnp.int32, s.shape, 1)
            qry = jnp.where(qry >= tq, qry - tq, qry)
            s = jnp.where(key <= qry, s, -jnp.inf)
        s_ref[n % 2] = s
        return jnp.max(s, axis=0, keepdims=True)

    m = l = None
    mx = scores(0)
    for n, (qi, kb) in enumerate(steps):
        mx_next = scores(n + 1) if n + 1 < len(steps) else None
        slot = qi % 2
        if kb == qi:
            m = mx
            p = jnp.exp2(s_ref[n % 2] - m)
            l = jnp.sum(p, axis=0, keepdims=True)
            acc_ref[slot] = _dot(vt_ref[kb], p.astype(BF16))
        else:
            m_new = jnp.maximum(m, mx)
            alpha = jnp.exp2(m - m_new)
            p = jnp.exp2(s_ref[n % 2] - m_new)
            l = alpha * l + jnp.sum(p, axis=0, keepdims=True)
            acc_ref[slot] = alpha * acc_ref[slot] + _dot(vt_ref[kb], p.astype(BF16))
            m = m_new
        if n + 1 == len(steps) or steps[n + 1][0] != qi:
            rows = slice(qi * tq, (qi + 1) * tq)
            o_t = acc_ref[slot] / l
            out = _diff_finish(o_t[:, :tq].T, o_t[:, tq:].T, lam_full, lam_init, sub_ref[...],
                               ga_ref[rows, :].astype(F32))
            o_ref[rows, :] = out.astype(o_ref.dtype)
        mx = mx_next


def _attn_flat(q, kbf, vt, ga, lam, sub_w, n_seq, t_seq, tq, lam_init):
    nq = t_seq // tq
    kern = functools.partial(_attn_flat_kernel, tq=tq, nq=nq, lam_init=lam_init)
    seq_head = lambda b, h: (b, h)
    c2 = lambda b, h: (0, 0)
    return pl.pallas_call(
        kern,
        grid=(n_seq, DA_HEADS),
        in_specs=[
            pl.BlockSpec((t_seq, LANES), seq_head),
            pl.BlockSpec((t_seq, LANES), seq_head),
            pl.BlockSpec((None, None, nq, DA_V_DIM, tq), lambda b, h: (b, h, 0, 0, 0)),
            pl.BlockSpec((t_seq, LANES), seq_head),
            pl.BlockSpec(lam.shape, c2),
            pl.BlockSpec(sub_w.shape, c2),
        ],
        out_specs=pl.BlockSpec((t_seq, LANES), seq_head),
        out_shape=jax.ShapeDtypeStruct((n_seq * t_seq, DA_WIDTH), BF16),
        scratch_shapes=[pltpu.VMEM((2, DA_V_DIM, 2 * tq), F32), pltpu.VMEM((2, tq, 2 * tq), F32),
                        pltpu.VMEM((2, 2 * tq, LANES), BF16)],
        compiler_params=_cparams(("arbitrary", "arbitrary")),
        name="attn_prompt",
    )(q, kbf, vt, ga, lam, sub_w)


def _attn_sample_kernel(*refs, t_new, n_pages, lam_init):
    pt_ref, q_ref = refs[0], refs[1]
    k_refs = refs[2:2 + n_pages]
    v_refs = refs[2 + n_pages:2 + 2 * n_pages]
    (kn_ref, vn_ref, ga_ref, lam_ref, sub_ref, bias_ref, biasn_ref, o_ref, s_ref) = refs[2 + 2 * n_pages:]
    del pt_ref
    n_rows = 2 * DA_HEADS * t_new

    pieces = [None] * (2 * DA_HEADS)
    for h in range(DA_HEADS):
        st = _stack_maps(q_ref[:, h * LANES:(h + 1) * LANES])
        pieces[h] = st[:t_new]
        pieces[DA_HEADS + h] = st[t_new:]
    wq = jnp.concatenate(pieces, axis=0).astype(BF16)

    bias = bias_ref[...]
    mx = None
    for p in range(n_pages):
        s = _dot_nt(wq, k_refs[p][...].astype(BF16)) + bias
        s_ref[p] = s
        mx = s if mx is None else jnp.maximum(mx, s)
    kn = kn_ref[...]
    vn = vn_ref[...]
    sn = _dot_nt(wq, kn.astype(BF16)) + biasn_ref[...]
    m = jnp.maximum(jnp.max(mx, axis=1, keepdims=True), jnp.max(sn, axis=1, keepdims=True))

    pn = jnp.exp(sn - m)
    acc = _dot(pn.astype(BF16), vn.astype(BF16))
    lsum = None
    for p in range(n_pages):
        pe = jnp.exp(s_ref[p] - m)
        lsum = pe if lsum is None else lsum + pe
        acc = acc + _dot(pe.astype(BF16), v_refs[p][...].astype(BF16))
    l = jnp.sum(lsum, axis=1, keepdims=True) + jnp.sum(pn, axis=1, keepdims=True)

    half = n_rows // 2
    o1 = acc[:half] / l[:half]
    o2 = acc[half:] / l[half:]
    gate = jnp.concatenate([ga_ref[:, h * LANES:(h + 1) * LANES] for h in range(DA_HEADS)], axis=0)
    out = _diff_finish(o1, o2, _lambda_full(lam_ref, lam_init), lam_init, sub_ref[...], gate)
    for h in range(DA_HEADS):
        o_ref[:, h * LANES:(h + 1) * LANES] = out[h * t_new:(h + 1) * t_new].astype(o_ref.dtype)


def _sample_biases(t_new, page_rows):
    n_rows = 2 * DA_HEADS * t_new
    row = np.arange(n_rows)[:, None]
    row_head, row_q = (row // t_new) % DA_HEADS, row % t_new
    col = np.arange(page_rows)[None, :]
    bias = np.where(row_head == col % DA_HEADS, 0.0, -np.inf).astype(np.float32)
    coln = np.arange(DA_HEADS * t_new)[None, :]
    biasn = np.where((row_head == coln % DA_HEADS) & (coln // DA_HEADS <= row_q), 0.0, -np.inf).astype(np.float32)
    return jnp.asarray(bias), jnp.asarray(biasn)


def _attn_sample(page_table_flat, q, ck, cv, k_new, v_new, ga, lam, sub_w, layer_j, n_seq, t_new, n_pages,
                 lam_init):
    page_rows = ck.shape[2]
    kern = functools.partial(_attn_sample_kernel, t_new=t_new, n_pages=n_pages, lam_init=lam_init)
    bias, biasn = _sample_biases(t_new, page_rows)
    seq_map = lambda n, pt: (n, 0)
    const2 = lambda n, pt: (0, 0)
    n_rows = 2 * DA_HEADS * t_new

    def page_spec(p):
        return pl.BlockSpec((None, None, page_rows, LANES), lambda n, pt: (pt[n * n_pages + p], layer_j, 0, 0))

    seq_spec = lambda: pl.BlockSpec((t_new, DA_WIDTH), seq_map)
    new_spec = lambda: pl.BlockSpec((None, None, t_new * DA_HEADS, LANES), lambda n, pt: (n, layer_j, 0, 0))
    grid_spec = pltpu.PrefetchScalarGridSpec(
        num_scalar_prefetch=1,
        grid=(n_seq,),
        in_specs=([seq_spec()] + [page_spec(p) for p in range(n_pages)] + [page_spec(p) for p in range(n_pages)]
                  + [new_spec(), new_spec(), seq_spec(), pl.BlockSpec(lam.shape, const2),
                     pl.BlockSpec(sub_w.shape, const2), pl.BlockSpec(bias.shape, const2),
                     pl.BlockSpec(biasn.shape, const2)]),
        out_specs=seq_spec(),
        scratch_shapes=[pltpu.VMEM((n_pages, n_rows, page_rows), F32)],
    )
    return pl.pallas_call(
        kern,
        grid_spec=grid_spec,
        out_shape=jax.ShapeDtypeStruct((n_seq * t_new, DA_WIDTH), F32),
        compiler_params=_cparams(("arbitrary",)),
        name="attn_sample",
    )(page_table_flat, q, *([ck] * n_pages), *([cv] * n_pages), k_new, v_new, ga, lam, sub_w, bias, biasn)


def _out_ln_kernel(a_ref, b_ref, x_ref, w_ref, g_ref, bb_ref, y_ref):
    half = a_ref.shape[1]
    for rs in _row_groups(a_ref.shape[0]):
        acc = (_dot(a_ref[rs, :].astype(BF16), w_ref[0:half, :])
               + _dot(b_ref[rs, :].astype(BF16), w_ref[half:2 * half, :]))
        y_ref[rs, :] = _layer_norm_rows(DN_ALPHA * x_ref[rs, :] + acc, g_ref[...], bb_ref[...])


def _out_ln(a, b, x2d, w_out, ln_g, ln_b, tm):
    total = x2d.shape[0]
    row = lambda i: (i, 0)
    const = lambda i: (0, 0)
    return pl.pallas_call(
        _out_ln_kernel,
        grid=(total // tm,),
        in_specs=[
            pl.BlockSpec((tm, a.shape[1]), row),
            pl.BlockSpec((tm, b.shape[1]), row),
            pl.BlockSpec((tm, D_MODEL), row),
            pl.BlockSpec(w_out.shape, const),
            pl.BlockSpec(ln_g.shape, const),
            pl.BlockSpec(ln_b.shape, const),
        ],
        out_specs=pl.BlockSpec((tm, D_MODEL), row),
        out_shape=jax.ShapeDtypeStruct((total, D_MODEL), F32),
        compiler_params=_cparams(("arbitrary",)),
        name="out_ln",
    )(a, b, x2d, w_out, ln_g, ln_b)


def _gla_constants(top):
    big = GLA_BLOCK
    idx = np.arange(big)
    levels = int(round(math.log2(top)))
    seg = [((idx[:, None] // top == idx[None, :] // top) & (idx[None, :] <= idx[:, None])).astype(np.float32)]
    pairs = []
    n = top
    for _ in range(levels):
        start = (idx // n) * n
        rho = start + n // 2 - 1
        upper = idx > rho
        t = idx[None, :]
        c = np.where(upper[:, None], (t > rho[:, None]) & (t <= idx[:, None]), (t > idx[:, None]) & (t <= rho[:, None]))
        seg.append(c.astype(np.float32))
        pairs.append(((start[:, None] == start[None, :]) & upper[:, None] & ~upper[None, :]).astype(np.float32))
        n //= 2
    seg = np.concatenate(seg, axis=0)
    return jnp.asarray(np.concatenate([seg, seg], axis=1), BF16), jnp.asarray(np.stack(pairs)), levels


def _gla_intra(q_h, k_h, seg_h, pair_ref, levels):
    big = GLA_BLOCK
    att = jnp.zeros((big, big), F32)
    for lv in range(levels):
        e = jnp.exp2(seg_h[(1 + lv) * big:(2 + lv) * big, :])
        att = att + _dot_nt((q_h * e).astype(BF16), (k_h * e).astype(BF16)) * pair_ref[lv]
    diag = jnp.sum(q_h * k_h, axis=1, keepdims=True)
    row = lax.broadcasted_iota(jnp.int32, (big, big), 0)
    col = lax.broadcasted_iota(jnp.int32, (big, big), 1)
    return att + jnp.where(row == col, diag, 0.0)


def _odd_kernel(*refs, tn, tr, levels, has_state, layer_j, n_alias):
    n_in = 11 + (1 if has_state else 0)
    refs = refs[:n_in] + refs[n_in + n_alias:]
    if has_state:
        (x_ref, w_ref, wga_ref, wgb_ref, bg_ref, nw_ref, wo_ref, lng_ref, lnb_ref, seg_ref, pair_ref,
         s0_ref, y_ref, sout_ref, q_s, k_s, v_s, g_s, o_s, st_s) = refs
    else:
        (x_ref, w_ref, wga_ref, wgb_ref, bg_ref, nw_ref, wo_ref, lng_ref, lnb_ref, seg_ref, pair_ref,
         y_ref, sout_ref, q_s, k_s, v_s, g_s, o_s, st_s) = refs
        s0_ref = None
    s_layer = _own_layer(sout_ref, layer_j, n_alias)
    r = pl.program_id(1)
    rows = tn * tr
    big = GLA_BLOCK
    kw = GLA_HEADS * GLA_K_DIM
    vw = GLA_HEADS * GLA_V_DIM
    x = x_ref[...]
    xb = x.astype(BF16)
    q_s[...] = _dot(xb, w_ref[:, 0:kw]) * (GLA_K_DIM ** -0.5)
    k_s[...] = _dot(xb, w_ref[:, kw:2 * kw])
    v_s[...] = _dot(xb, w_ref[:, 2 * kw:2 * kw + vw]).astype(BF16)
    low = _dot(xb, wga_ref[...])
    z = _dot(low.astype(BF16), wgb_ref[...]) + bg_ref[...]
    g_s[...] = (jnp.minimum(z, 0.0) - jnp.log1p(jnp.exp(-jnp.abs(z)))) * (math.log2(math.e) / GLA_TAU)

    if not has_state:
        @pl.when(r == 0)
        def _():
            st_s[...] = jnp.zeros(st_s.shape, F32)

    def block(c, carry):
        r0 = pl.multiple_of(c * big, big)
        qb = q_s[pl.ds(r0, big), :]
        kb = k_s[pl.ds(r0, big), :]
        gb = g_s[pl.ds(r0, big), :]
        hi = gb.astype(BF16)
        lo = (gb - hi.astype(F32)).astype(BF16)
        seg = _dot(seg_ref[...], jnp.concatenate([hi, lo], axis=0))
        for h in range(GLA_HEADS):
            ks = slice(h * GLA_K_DIM, (h + 1) * GLA_K_DIM)
            q_h, k_h, seg_h = qb[:, ks], kb[:, ks], seg[:, ks]
            v_h = v_s[pl.ds(r0, big), h * GLA_V_DIM:(h + 1) * GLA_V_DIM]
            att = _gla_intra(q_h, k_h, seg_h, pair_ref, levels)
            o_h = _dot(att.astype(BF16), v_h)
            cb = seg_h[0:big, :]
            qe = q_h * jnp.exp2(cb)
            if has_state:
                n_runs = big // tr
                v_hf = v_h.astype(F32)
                pad_k = jnp.zeros((2 * SUBLANES - tr, GLA_K_DIM), F32)
                pad_v = jnp.zeros((2 * SUBLANES - tr, GLA_V_DIM), F32)
                inter = []
                for s in range(n_runs):
                    rs = slice(s * tr, (s + 1) * tr)
                    seq = c * n_runs + s
                    st = s0_ref[seq, h]
                    qep = jnp.concatenate([qe[rs], pad_k], axis=0).astype(BF16)
                    inter.append(_dot(qep, st.astype(BF16))[0:tr])
                    cb_last = cb[(s + 1) * tr - 1:(s + 1) * tr, :]
                    kt = k_h[rs] * jnp.exp2(cb_last - cb[rs])
                    ktp = jnp.concatenate([kt, pad_k], axis=0).astype(BF16)
                    vtp = jnp.concatenate([v_hf[rs], pad_v], axis=0).astype(BF16)
                    dcol = jnp.broadcast_to(jnp.exp2(cb_last), (GLA_K_DIM, GLA_K_DIM)).T
                    decay = jnp.concatenate([dcol] * (GLA_V_DIM // GLA_K_DIM), axis=1)
                    sout_ref[(seq,) + s_layer + (h,)] = st * decay + _dot_tn(ktp, vtp)
                o_h = o_h + jnp.concatenate(inter, axis=0)
            else:
                st = st_s[h]
                o_h = o_h + _dot_nt(qe.astype(BF16), st.astype(BF16))
                cb_last = cb[big - 1:big, :]
                kt = (k_h * jnp.exp2(cb_last - cb)).astype(BF16)
                st_s[h] = st * jnp.exp2(cb_last) + _dot_tn(v_h, kt)
            o_s[pl.ds(r0, big), h * GLA_V_DIM:(h + 1) * GLA_V_DIM] = o_h
        return carry

    lax.fori_loop(0, rows // big, block, 0, unroll=True)

    if not has_state:
        @pl.when(r == pl.num_programs(1) - 1)
        def _():
            for h in range(GLA_HEADS):
                sout_ref[(0,) + s_layer + (h,)] = st_s[h].T

    nw = nw_ref[...]
    for h in range(GLA_HEADS):
        vs = slice(h * GLA_V_DIM, (h + 1) * GLA_V_DIM)
        o_h = o_s[:, vs]
        ms = jnp.mean(o_h * o_h, axis=-1, keepdims=True)
        gate = _dot(xb, w_ref[:, 2 * kw + vw + h * GLA_V_DIM:2 * kw + vw + (h + 1) * GLA_V_DIM])
        o_s[:, vs] = o_h * lax.rsqrt(ms + RMS_EPS) * nw * _silu(gate)
    for rs in _row_groups(rows):
        acc = _dot(o_s[rs, :].astype(BF16), wo_ref[...])
        y_ref[rs, :] = _layer_norm_rows(DN_ALPHA * x_ref[rs, :] + acc, lng_ref[...], lnb_ref[...])


def _odd_layer(x2d, n_seq, t_seq, tn, tr, w_in, wga, wgb, bg, nw, wo, lng, lnb, state0, layer_j, n_odd,
               state_prev):
    has_state = state0 is not None
    seg, pair, levels = _gla_constants(tr if has_state else GLA_BLOCK)
    n_row_tiles = t_seq // tr
    rows = tn * tr
    total = n_seq * t_seq
    n_alias = 0 if state_prev is None else 1
    kern = functools.partial(_odd_kernel, tn=tn, tr=tr, levels=levels, has_state=has_state, layer_j=layer_j,
                             n_alias=n_alias)
    row_map = lambda n, r: (n * n_row_tiles + r, 0)
    c2 = lambda n, r: (0, 0)
    c3 = lambda n, r: (0, 0, 0)
    st_map = lambda n, r: (n, layer_j, 0, 0, 0)
    in_specs = [
        pl.BlockSpec((rows, D_MODEL), row_map),
        pl.BlockSpec(w_in.shape, c2), pl.BlockSpec(wga.shape, c2), pl.BlockSpec(wgb.shape, c2),
        pl.BlockSpec(bg.shape, c2), pl.BlockSpec(nw.shape, c2), pl.BlockSpec(wo.shape, c2),
        pl.BlockSpec(lng.shape, c2), pl.BlockSpec(lnb.shape, c2),
        pl.BlockSpec(seg.shape, c2), pl.BlockSpec(pair.shape, c3),
    ]
    args = [x2d, w_in, wga, wgb, bg, nw, wo, lng, lnb, seg, pair]
    st_block = (tn, None, GLA_HEADS, GLA_K_DIM, GLA_V_DIM)
    if has_state:
        in_specs.append(pl.BlockSpec(st_block, st_map))
        args.append(state0)
    aliases = {}
    if state_prev is None:
        out_state = pl.BlockSpec((tn, n_odd, GLA_HEADS, GLA_K_DIM, GLA_V_DIM), lambda n, r: (n, 0, 0, 0, 0))
    else:
        out_state = pl.BlockSpec(st_block, st_map)
        aliases = {len(args): 1}
        in_specs.append(pl.BlockSpec(memory_space=pl.ANY))
        args.append(state_prev)
    kw = GLA_HEADS * GLA_K_DIM
    vw = GLA_HEADS * GLA_V_DIM
    return pl.pallas_call(
        kern,
        grid=(n_seq // tn, n_row_tiles),
        in_specs=in_specs,
        out_specs=[pl.BlockSpec((rows, D_MODEL), row_map), out_state],
        out_shape=[jax.ShapeDtypeStruct((total, D_MODEL), F32),
                   jax.ShapeDtypeStruct((n_seq, n_odd, GLA_HEADS, GLA_K_DIM, GLA_V_DIM), F32)],
        input_output_aliases=aliases,
        scratch_shapes=[pltpu.VMEM((rows, kw), F32), pltpu.VMEM((rows, kw), F32), pltpu.VMEM((rows, vw), BF16),
                        pltpu.VMEM((rows, kw), F32), pltpu.VMEM((rows, vw), F32),
                        pltpu.VMEM((GLA_HEADS, GLA_V_DIM, GLA_K_DIM), F32)],
        compiler_params=_cparams(("arbitrary", "arbitrary")),
        name="odd_layer",
    )(*args)


def _rope_tables(positions):
    half = ROT_DIM // 2
    inv_freq = jnp.power(jnp.float32(ROPE_THETA), -jnp.arange(0, ROT_DIM, 2, dtype=jnp.float32) / ROT_DIM)
    ang = positions.astype(jnp.float32)[:, None] * inv_freq[None, :]
    cos, sin = jnp.cos(ang), jnp.sin(ang)
    n = positions.shape[0]
    ones = jnp.ones((n, DA_QK_DIM - ROT_DIM), F32)
    zeros = jnp.zeros((n, DA_QK_DIM - ROT_DIM), F32)
    zh = jnp.zeros((n, half), F32)
    cos_map = jnp.concatenate([cos, cos, ones], axis=1)
    sina_map = jnp.concatenate([zh, sin, zeros], axis=1)
    sinb_map = jnp.concatenate([-sin, zh, zeros], axis=1)
    two = lambda t: jnp.concatenate([t, t], axis=1)
    return two(cos_map), two(sina_map), two(sinb_map)


def kernel(x_prompt, x_sample, cache_k, cache_v, state_pool, state_gla, page_table, w_in_even, w_pool_lin,
           pool_scale, diff_lambda_params, subln_w, w_out_even, w_in_odd, w_gate_a, w_gate_b, b_gate, gla_norm_w,
           w_out_odd, ln_g, ln_b):
    n_p, t_p, _ = x_prompt.shape
    n_s, t_s, _ = x_sample.shape
    n_pool, n_even, page_size = cache_k.shape[:3]
    n_pages = page_table.shape[1]
    past_len = n_pages * page_size

    tr_p = min(512, t_p)
    tq_p = min(512, t_p)
    tn_even_s = 16
    tn_odd_s = GLA_BLOCK // t_s

    w_in_even_b = w_in_even.astype(BF16)
    w_pool_b = w_pool_lin.astype(BF16)
    w_out_even_b = w_out_even.astype(BF16)
    w_in_odd_b = w_in_odd.astype(BF16)
    w_out_odd_b = w_out_odd.astype(BF16)
    pad_rank = LANES - GLA_GATE_RANK
    w_ga_b = jnp.pad(w_gate_a, ((0, 0), (0, 0), (0, pad_rank))).astype(BF16)
    w_gb_b = jnp.pad(w_gate_b, ((0, 0), (0, pad_rank), (0, 0))).astype(BF16)

    cos_p, sina_p, sinb_p = _rope_tables(jnp.arange(t_p))
    cos_s, sina_s, sinb_s = (jnp.tile(t, (tn_even_s, 1)) for t in _rope_tables(past_len + jnp.arange(t_s)))

    ck = cache_k.reshape(n_pool, n_even, page_size * DA_HEADS, 2 * DA_QK_DIM)
    cv = cache_v.reshape(n_pool, n_even, page_size * DA_HEADS, DA_V_DIM)
    pt_flat = page_table.reshape(-1).astype(jnp.int32)
    pool16 = jnp.pad(state_pool, ((0, 0), (0, 0), (POOL_HIST - state_pool.shape[2], 0), (0, 0)))

    yp = x_prompt.reshape(n_p * t_p, D_MODEL)
    ys = x_sample.reshape(n_s * t_s, D_MODEL)
    n_odd = state_gla.shape[1]
    pp, psm = [], []
    kv_p = kv_s = gla_p = gla_s = None
    for i in range(DEPTH):
        j = i // 2
        lng, lnb = ln_g[i][None, :], ln_b[i][None, :]
        if i % 2 == 0:
            lam_init = 0.8 - 0.6 * math.exp(-0.3 * i)
            pscale = pool_scale[j][None, :]
            sub_w = subln_w[j][None, :]
            lam = diff_lambda_params[j]
            pmix, q, k32, v32, ga, npool, kbf, vt = _even_in(
                yp, n_p, t_p, 1, tr_p, tq_p, 0, w_in_even_b[j], w_pool_b[j], pscale, cos_p, sina_p, sinb_p,
                None, j, n_even, kv_p, BF16)
            kv_p = (k32, v32)
            omix = _attn_flat(q, kbf, vt, ga, lam, sub_w, n_p, t_p, tq_p, lam_init)
            yp = _out_ln(pmix, omix, yp, w_out_even_b[j], lng, lnb, min(2 * tr_p, t_p))
            pp.append(npool[:, 1:])
            pmix, q, k32, v32, ga, npool = _even_in(
                ys, n_s, t_s, tn_even_s, t_s, t_s, past_len, w_in_even_b[j], w_pool_b[j], pscale, cos_s, sina_s,
                sinb_s, pool16, j, n_even, kv_s, F32)
            kv_s = (k32, v32)
            omix = _attn_sample(pt_flat, q, ck, cv, k32, v32, ga, lam, sub_w, j, n_s, t_s, n_pages, lam_init)
            ys = _out_ln(pmix, omix, ys, w_out_even_b[j], lng, lnb, min(512, n_s * t_s))
            psm.append(npool[:, 1:])
        else:
            ow = (w_in_odd_b[j], w_ga_b[j], w_gb_b[j], b_gate[j][None, :], gla_norm_w[j][None, :], w_out_odd_b[j],
                  lng, lnb)
            yp, gla_p = _odd_layer(yp, n_p, t_p, 1, tr_p, *ow, None, j, n_odd, gla_p)
            ys, gla_s = _odd_layer(ys, n_s, t_s, tn_odd_s, t_s, *ow, state_gla, j, n_odd, gla_s)
    kv5 = lambda a, n, t: a.reshape(n, n_even, t, DA_HEADS, LANES)
    return (yp.reshape(n_p, t_p, D_MODEL), ys.reshape(n_s, t_s, D_MODEL),
            kv5(kv_p[0], n_p, t_p), kv5(kv_p[1], n_p, t_p), jnp.stack(pp, axis=1), gla_p,
            kv5(kv_s[0], n_s, t_s), kv5(kv_s[1], n_s, t_s), jnp.stack(psm, axis=1), gla_s)
```
